```python
import math, functools
import jax, jax.numpy as jnp
from jax import lax
import numpy as np

D_MODEL = 1024
BATCH = 8
SEQ = 4096
DEPTH = 1
DEC_BATCH = 32
DEC_SEQ = 1
PAST_LEN = 16384
PAGE_SIZE = 128

N_HEADS = 8
HEAD_DIM = 64
ATTN_WIDTH = N_HEADS * HEAD_DIM
SSM_WIDTH = 512
SSM_GROUP = 16
N_GROUPS = SSM_WIDTH // SSM_GROUP
SSM_STATE = 64
N_EXPERTS = 32
TOP_K = 4
D_FF = 1024
SWIGLU_LIMIT = 7.0
SWIGLU_ALPHA = 1.702
Q_BLOCK = 128
RMS_EPS = 1e-5
SB_BIAS_INIT = -8.0
IN_WIDTH = 3 * ATTN_WIDTH + SSM_WIDTH + 2 * D_MODEL
SPLITS = (ATTN_WIDTH, 2 * ATTN_WIDTH, 3 * ATTN_WIDTH, 3 * ATTN_WIDTH + SSM_WIDTH,
          3 * ATTN_WIDTH + SSM_WIDTH + D_MODEL)

kernel_name = "stickbreak_s5_moe_hybrid_step"


def rmsnorm(x, g):
    x32 = x.astype(jnp.float32)
    y = x32 * lax.rsqrt(jnp.mean(x32 * x32, axis=-1, keepdims=True) + RMS_EPS)
    return (y * g.astype(jnp.float32)).astype(x.dtype)


def stick_breaking(q, k, v, bias, q_pos, k_pos):
    z = jnp.einsum("bqhd,bkhd->bhqk", q, k, preferred_element_type=jnp.float32) * (HEAD_DIM ** -0.5)
    z = z + bias.astype(jnp.float32)[None, :, None, None]
    mask = k_pos[None, :] < q_pos[:, None]
    log_1m = jnp.where(mask, jax.nn.log_sigmoid(-z), 0.0)
    suffix = lax.cumsum(log_1m, axis=3, reverse=True) - log_1m
    w = jnp.where(mask, jnp.exp(jax.nn.log_sigmoid(z) + suffix), 0.0)
    return jnp.einsum("bhqk,bkhd->bqhd", w.astype(v.dtype), v)


def attend_prompt(q, k, v, bias):
    nb, nt = q.shape[0], q.shape[1]
    n_blk = nt // Q_BLOCK
    q_blocks = jnp.moveaxis(q.reshape(nb, n_blk, Q_BLOCK, N_HEADS, HEAD_DIM), 1, 0)
    k_pos = jnp.arange(nt)

    def one_block(args):
        q_blk, blk = args
        q_pos = blk * Q_BLOCK + jnp.arange(Q_BLOCK)
        return stick_breaking(q_blk, k, v, bias, q_pos, k_pos)

    out = lax.map(one_block, (q_blocks, jnp.arange(n_blk)))
    return jnp.moveaxis(out, 0, 1).reshape(nb, nt, N_HEADS, HEAD_DIM)


def attend_sample(q, k, v, bias, past_k, past_v):
    nt = q.shape[1]
    past = past_k.shape[1]
    k_all = jnp.concatenate([past_k.astype(k.dtype), k], axis=1)
    v_all = jnp.concatenate([past_v.astype(v.dtype), v], axis=1)
    q_pos = past + jnp.arange(nt)
    k_pos = jnp.arange(past + nt)
    return stick_breaking(q, k_all, v_all, bias, q_pos, k_pos)


def s5_scan(u, lam_re, lam_im, log_dt, b_re, b_im, c_re, c_im, d_skip, h0_re, h0_im):
    nb, nt = u.shape[0], u.shape[1]
    f32 = jnp.float32
    u32 = u.astype(f32).reshape(nb, nt, N_GROUPS, SSM_GROUP)
    lr, li = lam_re.astype(f32), lam_im.astype(f32)
    dt = jnp.exp(log_dt.astype(f32))[:, None]
    mag = jnp.exp(lr * dt)
    lb_re, lb_im = mag * jnp.cos(li * dt), mag * jnp.sin(li * dt)
    den = lr * lr + li * li
    nr, ni = lb_re - 1.0, lb_im
    f_re = (nr * lr + ni * li) / den
    f_im = (ni * lr - nr * li) / den
    br, bi = b_re.astype(f32), b_im.astype(f32)
    bb_re = f_re[..., None] * br - f_im[..., None] * bi
    bb_im = f_re[..., None] * bi + f_im[..., None] * br
    drv_re = jnp.einsum("gph,btgh->btgp", bb_re, u32)
    drv_im = jnp.einsum("gph,btgh->btgp", bb_im, u32)
    h0r, h0i = h0_re.astype(f32), h0_im.astype(f32)
    drv_re = drv_re.at[:, 0].add(lb_re * h0r - lb_im * h0i)
    drv_im = drv_im.at[:, 0].add(lb_re * h0i + lb_im * h0r)
    a_re = jnp.broadcast_to(lb_re, drv_re.shape)
    a_im = jnp.broadcast_to(lb_im, drv_im.shape)

    def combine(e1, e2):
        a1r, a1i, b1r, b1i = e1
        a2r, a2i, b2r, b2i = e2
        return (a1r * a2r - a1i * a2i,
                a1r * a2i + a1i * a2r,
                a2r * b1r - a2i * b1i + b2r,
                a2r * b1i + a2i * b1r + b2i)

    _, _, h_re, h_im = lax.associative_scan(combine, (a_re, a_im, drv_re, drv_im), axis=1)
    y = (jnp.einsum("ghp,btgp->btgh", c_re.astype(f32), h_re)
         - jnp.einsum("ghp,btgp->btgh", c_im.astype(f32), h_im))
    y = y + d_skip.astype(f32).reshape(N_GROUPS, SSM_GROUP) * u32
    return y.reshape(nb, nt, SSM_WIDTH), h_re[:, -1], h_im[:, -1]


def routed_ffn(x, router_w, router_b, w_up, b_up, w_down, b_down):
    shp = x.shape
    t = x.reshape(-1, D_MODEL)
    logits = (t @ router_w + router_b).astype(jnp.float32)
    top_val, top_idx = lax.top_k(logits, TOP_K)
    gates = jax.nn.softmax(top_val, axis=-1)
    combine = jnp.sum(jax.nn.one_hot(top_idx, N_EXPERTS, dtype=jnp.float32) * gates[..., None], axis=1)
    out = jnp.zeros(t.shape, jnp.float32)
    for e in range(N_EXPERTS):
        hu = t @ w_up[e] + b_up[e]
        x_glu = jnp.minimum(hu[:, 0::2], SWIGLU_LIMIT)
        x_lin = jnp.clip(hu[:, 1::2], -SWIGLU_LIMIT, SWIGLU_LIMIT)
        act = x_glu * jax.nn.sigmoid(SWIGLU_ALPHA * x_glu) * (x_lin + 1.0)
        out = out + combine[:, e:e + 1] * (act @ w_down[e] + b_down[e]).astype(jnp.float32)
    return out.astype(x.dtype).reshape(shp)


def hybrid_layer(x, attend, h0_re, h0_im, p):
    (norm_mix, w_in, sb_bias, w_attn_up, lam_re, lam_im, log_dt, b_re, b_im, c_re, c_im, d_skip,
     glu_w1, glu_b1, glu_w2, glu_b2, w_out, norm_ffn, router_w, router_b,
     moe_w_up, moe_b_up, moe_w_down, moe_b_down) = p
    nb, nt = x.shape[0], x.shape[1]
    xn = rmsnorm(x, norm_mix)
    proj = xn @ w_in
    q, k, v, u, g_attn, g_ssm = jnp.split(proj, SPLITS, axis=-1)
    heads = (nb, nt, N_HEADS, HEAD_DIM)
    q, k, v = q.reshape(heads), k.reshape(heads), v.reshape(heads)
    y_attn = attend(q, k, v, sb_bias).reshape(nb, nt, ATTN_WIDTH) @ w_attn_up
    y_s, h_re, h_im = s5_scan(u, lam_re, lam_im, log_dt, b_re, b_im, c_re, c_im, d_skip, h0_re, h0_im)
    z = jax.nn.gelu(y_s).astype(x.dtype)
    y_ssm = (z @ glu_w1 + glu_b1) * jax.nn.sigmoid(z @ glu_w2 + glu_b2)
    merged = jax.nn.sigmoid(g_attn) * y_attn + jax.nn.sigmoid(g_ssm) * y_ssm
    h = x + merged @ w_out
    h = h + routed_ffn(rmsnorm(h, norm_ffn), router_w, router_b, moe_w_up, moe_b_up, moe_w_down, moe_b_down)
    return h, k, v, h_re, h_im


def setup_inputs(seed: int = 0) -> dict:
    key = jax.random.key(seed)
    ks = jax.random.split(key, 32)
    f32 = jnp.float32
    L = DEPTH
    n_pages = PAST_LEN // PAGE_SIZE
    n_pool = (DEC_BATCH * n_pages * 5) // 4

    def nrm(k, shape, scale):
        return scale * jax.random.normal(k, shape, f32)

    gp = (L, N_GROUPS, SSM_STATE)
    return {
        "x_prompt": nrm(ks[0], (BATCH, SEQ, D_MODEL), 1.0),
        "x_sample": nrm(ks[1], (DEC_BATCH, DEC_SEQ, D_MODEL), 1.0),
        "cache_k": nrm(ks[2], (L, n_pool, PAGE_SIZE, N_HEADS, HEAD_DIM), 1.0),
        "cache_v": nrm(ks[3], (L, n_pool, PAGE_SIZE, N_HEADS, HEAD_DIM), 1.0),
        "state_ssm_re": nrm(ks[4], (L, DEC_BATCH, N_GROUPS, SSM_STATE), 0.1),
        "state_ssm_im": nrm(ks[5], (L, DEC_BATCH, N_GROUPS, SSM_STATE), 0.1),
        "page_table": jax.random.permutation(ks[6], n_pool)[: DEC_BATCH * n_pages]
                          .reshape(DEC_BATCH, n_pages).astype(jnp.int32),
        "norm_mix": 1.0 + nrm(ks[7], (L, D_MODEL), 0.02),
        "w_in": nrm(ks[8], (L, D_MODEL, IN_WIDTH), D_MODEL ** -0.5),
        "sb_bias": SB_BIAS_INIT + nrm(ks[31], (L, N_HEADS), 0.1),
        "w_attn_up": nrm(ks[9], (L, ATTN_WIDTH, D_MODEL), ATTN_WIDTH ** -0.5),
        "ssm_lambda_re": -0.5 * jnp.exp(nrm(ks[10], gp, 0.05)),
        "ssm_lambda_im": math.pi * jnp.arange(SSM_STATE, dtype=f32) + nrm(ks[11], gp, 0.01),
        "ssm_log_dt": jax.random.uniform(ks[12], (L, N_GROUPS), f32, math.log(1e-3), math.log(1e-1)),
        "ssm_b_re": nrm(ks[13], (L, N_GROUPS, SSM_STATE, SSM_GROUP), (2 * SSM_GROUP) ** -0.5),
        "ssm_b_im": nrm(ks[14], (L, N_GROUPS, SSM_STATE, SSM_GROUP), (2 * SSM_GROUP) ** -0.5),
        "ssm_c_re": nrm(ks[15], (L, N_GROUPS, SSM_GROUP, SSM_STATE), SSM_STATE ** -0.5),
        "ssm_c_im": nrm(ks[16], (L, N_GROUPS, SSM_GROUP, SSM_STATE), SSM_STATE ** -0.5),
        "ssm_d": nrm(ks[17], (L, SSM_WIDTH), 1.0),
        "glu_w1": nrm(ks[18], (L, SSM_WIDTH, D_MODEL), SSM_WIDTH ** -0.5),
        "glu_b1": nrm(ks[19], (L, D_MODEL), 0.01),
        "glu_w2": nrm(ks[20], (L, SSM_WIDTH, D_MODEL), SSM_WIDTH ** -0.5),
        "glu_b2": nrm(ks[21], (L, D_MODEL), 0.01),
        "w_out": nrm(ks[22], (L, D_MODEL, D_MODEL), D_MODEL ** -0.5),
        "norm_ffn": 1.0 + nrm(ks[23], (L, D_MODEL), 0.02),
        "router_w": nrm(ks[24], (L, D_MODEL, N_EXPERTS), D_MODEL ** -0.5),
        "router_b": nrm(ks[25], (L, N_EXPERTS), 0.01),
        "moe_w_up": nrm(ks[26], (L, N_EXPERTS, D_MODEL, 2 * D_FF), D_MODEL ** -0.5),
        "moe_b_up": nrm(ks[27], (L, N_EXPERTS, 2 * D_FF), 0.01),
        "moe_w_down": nrm(ks[28], (L, N_EXPERTS, D_FF, D_MODEL), D_FF ** -0.5),
        "moe_b_down": nrm(ks[29], (L, N_EXPERTS, D_MODEL), 0.01),
        "norm_final": 1.0 + nrm(ks[30], (D_MODEL,), 0.02),
    }


def reference(x_prompt, x_sample, cache_k, cache_v, state_ssm_re, state_ssm_im, page_table,
              norm_mix, w_in, sb_bias, w_attn_up, ssm_lambda_re, ssm_lambda_im, ssm_log_dt,
              ssm_b_re, ssm_b_im, ssm_c_re, ssm_c_im, ssm_d, glu_w1, glu_b1, glu_w2, glu_b2,
              w_out, norm_ffn, router_w, router_b, moe_w_up, moe_b_up, moe_w_down, moe_b_down,
              norm_final):
    past_len = page_table.shape[1] * PAGE_SIZE
    n_dec = x_sample.shape[0]
    h_p, h_s = x_prompt, x_sample
    kp_l, vp_l, srp_l, sip_l = [], [], [], []
    ks_l, vs_l, srs_l, sis_l = [], [], [], []
    for l in range(DEPTH):
        p = (norm_mix[l], w_in[l], sb_bias[l], w_attn_up[l], ssm_lambda_re[l], ssm_lambda_im[l],
             ssm_log_dt[l], ssm_b_re[l], ssm_b_im[l], ssm_c_re[l], ssm_c_im[l], ssm_d[l],
             glu_w1[l], glu_b1[l], glu_w2[l], glu_b2[l], w_out[l], norm_ffn[l],
             router_w[l], router_b[l], moe_w_up[l], moe_b_up[l], moe_w_down[l], moe_b_down[l])
        zeros = jnp.zeros((x_prompt.shape[0], N_GROUPS, SSM_STATE), jnp.float32)
        h_p, kp, vp, srp, sip = hybrid_layer(h_p, attend_prompt, zeros, zeros, p)
        past_k = cache_k[l][page_table].reshape(n_dec, past_len, N_HEADS, HEAD_DIM)
        past_v = cache_v[l][page_table].reshape(n_dec, past_len, N_HEADS, HEAD_DIM)
        attend = functools.partial(attend_sample, past_k=past_k, past_v=past_v)
        h_s, ks_, vs_, srs, sis = hybrid_layer(h_s, attend, state_ssm_re[l], state_ssm_im[l], p)
        kp_l.append(kp); vp_l.append(vp); srp_l.append(srp); sip_l.append(sip)
        ks_l.append(ks_); vs_l.append(vs_); srs_l.append(srs); sis_l.append(sis)
    y_prompt = rmsnorm(h_p, norm_final)
    y_sample = rmsnorm(h_s, norm_final)
    return (y_prompt, y_sample,
            jnp.stack(kp_l), jnp.stack(vp_l), jnp.stack(srp_l), jnp.stack(sip_l),
            jnp.stack(ks_l), jnp.stack(vs_l), jnp.stack(srs_l), jnp.stack(sis_l))
```

```python
import functools
import math

import jax
import jax.numpy as jnp
from jax import lax
from jax.experimental import pallas as pl
from jax.experimental.pallas import tpu as pltpu

F32 = jnp.float32
BF16 = jnp.bfloat16

D_MODEL = 1024
N_HEADS = 8
HEAD_DIM = 64
ATTN_WIDTH = N_HEADS * HEAD_DIM
SSM_WIDTH = 512
SSM_GROUP = 16
N_GROUPS = SSM_WIDTH // SSM_GROUP
SSM_STATE = 64
SSM_FLAT = N_GROUPS * SSM_STATE
N_EXPERTS = 32
TOP_K = 4
D_FF = 1024
SWIGLU_LIMIT = 7.0
SWIGLU_ALPHA = 1.702
RMS_EPS = 1e-5
PAGE_SIZE = 128
IN_WIDTH = 3 * ATTN_WIDTH + SSM_WIDTH + 2 * D_MODEL

LANES = 128
ROW_BLOCK = 256
ATTN_BLOCK = 256
HEADS_PER_STEP = 4
S5_CHUNKS = SSM_WIDTH // LANES
S5_TIME_BLOCK = 128
MOE_TILE = 512
MOE_FF_CHUNK = 256
PAGES_PER_STEP = 8
NEG_BIG = -1e30
VMEM_LIMIT = 56 * 1024 * 1024


def _dot(a, b):
    return jnp.dot(a, b, preferred_element_type=F32)


def _dot_nt(a, b):
    return lax.dot_general(a, b, (((1,), (1,)), ((), ())), preferred_element_type=F32)


def _softplus(z):
    return jnp.maximum(z, 0.0) + jnp.log(1.0 + jnp.exp(-jnp.abs(z)))


def _split_bf16(x):
    hi = x.astype(BF16)
    lo = (x - hi.astype(F32)).astype(BF16)
    return hi, lo


def _params(sem, vmem=VMEM_LIMIT):
    return pltpu.CompilerParams(dimension_semantics=sem, vmem_limit_bytes=vmem)


def _inproj_body(x_ref, nw_ref, w_ref, q_ref, k_ref, v_ref, kb_ref, vb_ref, u_ref, g_ref):
    x = x_ref[...]
    ms = jnp.mean(x * x, axis=-1, keepdims=True)
    xn = (x * lax.rsqrt(ms + RMS_EPS) * nw_ref[...]).astype(BF16)
    a = ATTN_WIDTH
    q_ref[...] = (_dot(xn, w_ref[:, 0:a]) * (HEAD_DIM ** -0.5)).astype(BF16)
    k = _dot(xn, w_ref[:, a:2 * a])
    k_ref[...] = k
    kb_ref[...] = k.astype(BF16)
    v = _dot(xn, w_ref[:, 2 * a:3 * a])
    v_ref[...] = v
    vb_ref[...] = v.astype(BF16)
    u_ref[...] = _dot(xn, w_ref[:, 3 * a:3 * a + SSM_WIDTH])
    g_ref[...] = _dot(xn, w_ref[:, 3 * a + SSM_WIDTH:])


def _inproj(x2d, norm_w, w_in_bf):
    n = x2d.shape[0]
    tm = ROW_BLOCK
    row = lambda i: (i, 0)
    full = lambda i: (0, 0)
    a = ATTN_WIDTH
    outs = [
        jax.ShapeDtypeStruct((n, a), BF16),
        jax.ShapeDtypeStruct((n, a), F32),
        jax.ShapeDtypeStruct((n, a), F32),
        jax.ShapeDtypeStruct((n, a), BF16),
        jax.ShapeDtypeStruct((n, a), BF16),
        jax.ShapeDtypeStruct((n, SSM_WIDTH), F32),
        jax.ShapeDtypeStruct((n, 2 * D_MODEL), F32),
    ]
    return pl.pallas_call(
        _inproj_body,
        out_shape=outs,
        grid=(n // tm,),
        in_specs=[pl.BlockSpec((tm, D_MODEL), row),
                  pl.BlockSpec((1, D_MODEL), full),
                  pl.BlockSpec((D_MODEL, IN_WIDTH), full)],
        out_specs=[pl.BlockSpec((tm, a), row)] * 5
                  + [pl.BlockSpec((tm, SSM_WIDTH), row), pl.BlockSpec((tm, 2 * D_MODEL), row)],
        compiler_params=_params(("parallel",)),
        name="inproj",
    )(x2d, norm_w, w_in_bf)


def _attn_body(bias_ref, q_ref, k_ref, v_ref, o_ref, acc_ref):
    tq = tk = ATTN_BLOCK
    hg = pl.program_id(1)
    qi = pl.program_id(2)
    width = HEADS_PER_STEP * HEAD_DIM
    q = q_ref[...]
    lane_head = lax.broadcasted_iota(jnp.int32, (1, width), 1) // HEAD_DIM
    r = lax.broadcasted_iota(jnp.int32, (tq, tk), 0)
    c = lax.broadcasted_iota(jnp.int32, (tq, tk), 1)
    later = (r > c).astype(BF16)
    causal = c < r
    qh = [jnp.where(lane_head == h, q, jnp.zeros_like(q)) for h in range(HEADS_PER_STEP)]
    bias = [bias_ref[hg * HEADS_PER_STEP + h] for h in range(HEADS_PER_STEP)]
    acc_ref[...] = jnp.zeros_like(acc_ref)

    def block(kb, carries, diagonal):
        start = pl.multiple_of(kb * tk, tk)
        kblk = k_ref[pl.ds(start, tk), :]
        vblk = v_ref[pl.ds(start, tk), :]
        new = []
        for h in range(HEADS_PER_STEP):
            z = _dot_nt(qh[h], kblk) + bias[h]
            sp = _softplus(z)
            if diagonal:
                sp = jnp.where(causal, sp, 0.0)
            hi, lo = _split_bf16(sp)
            newer = _dot(hi, later) + _dot(lo, later)
            w = jnp.exp(z - sp - newer - carries[h])
            if diagonal:
                w = jnp.where(causal, w, 0.0)
            vh = jnp.where(lane_head == h, vblk, jnp.zeros_like(vblk))
            acc_ref[...] += _dot(w.astype(BF16), vh)
            new.append(carries[h] + jnp.sum(sp, axis=-1, keepdims=True))
        return tuple(new)

    zero = tuple(jnp.zeros((tq, 1), F32) for _ in range(HEADS_PER_STEP))
    carries = block(qi, zero, True)
    lax.fori_loop(0, qi, lambda j, cs: block(qi - 1 - j, cs, False), carries)
    o_ref[...] = acc_ref[...].astype(o_ref.dtype)


def _attn_prompt(q, kb, vb, bias, nb, nt):
    tq = ATTN_BLOCK
    nq = nt // tq
    width = HEADS_PER_STEP * HEAD_DIM
    ng = ATTN_WIDTH // width
    return pl.pallas_call(
        _attn_body,
        out_shape=jax.ShapeDtypeStruct((nb * nt, ATTN_WIDTH), BF16),
        grid=(nb, ng, nq),
        in_specs=[pl.BlockSpec(memory_space=pltpu.SMEM),
                  pl.BlockSpec((tq, width), lambda b, g, i: (b * nq + i, g)),
                  pl.BlockSpec((nt, width), lambda b, g, i: (b, g)),
                  pl.BlockSpec((nt, width), lambda b, g, i: (b, g))],
        out_specs=pl.BlockSpec((tq, width), lambda b, g, i: (b * nq + i, g)),
        scratch_shapes=[pltpu.VMEM((tq, width), F32)],
        compiler_params=_params(("parallel", "parallel", "arbitrary")),
        name="attn_prompt",
    )(bias, q, kb, vb)


def _attn_sample_body(pt_ref, bias_ref, q_ref, *refs):
    np_ = PAGES_PER_STEP
    k_refs = refs[:np_]
    v_refs = refs[np_:2 * np_]
    o_ref, carry_ref, acc_ref = refs[2 * np_:]
    c = pl.program_id(1)

    @pl.when(c == 0)
    def _():
        carry_ref[...] = jnp.zeros_like(carry_ref)
        acc_ref[...] = jnp.zeros_like(acc_ref)

    head_of_lane = lax.broadcasted_iota(jnp.int32, (N_HEADS, ATTN_WIDTH), 1) // HEAD_DIM
    head_of_row = lax.broadcasted_iota(jnp.int32, (N_HEADS, ATTN_WIDTH), 0)
    own = head_of_lane == head_of_row
    qrow = q_ref[0].astype(F32)
    qbd = jnp.where(own, jnp.broadcast_to(qrow, (N_HEADS, ATTN_WIDTH)), 0.0).astype(BF16)
    r = lax.broadcasted_iota(jnp.int32, (PAGE_SIZE, PAGE_SIZE), 0)
    cc = lax.broadcasted_iota(jnp.int32, (PAGE_SIZE, PAGE_SIZE), 1)
    later = (r > cc).astype(BF16)
    bias = bias_ref[...]
    carry = carry_ref[...]
    acc = acc_ref[...]
    for i in range(np_):
        kp = k_refs[i][0].astype(BF16)
        vp = v_refs[i][0].astype(BF16)
        z = _dot_nt(qbd, kp) + bias
        sp = _softplus(z)
        hi, lo = _split_bf16(sp)
        newer = _dot(hi, later) + _dot(lo, later)
        w = jnp.exp(z - sp - newer - carry)
        acc = acc + _dot(w.astype(BF16), vp)
        carry = carry + jnp.sum(sp, axis=-1, keepdims=True)
    carry_ref[...] = carry
    acc_ref[...] = acc

    @pl.when(c == pl.num_programs(1) - 1)
    def _():
        o_ref[0] = jnp.sum(jnp.where(own, acc, 0.0), axis=0, keepdims=True).astype(o_ref.dtype)


def _attn_sample(q, cache_k, cache_v, page_table, bias):
    nseq, n_pages = page_table.shape
    np_ = PAGES_PER_STEP
    nchunk = n_pages // np_

    def page_map(i):
        def f(b, c, pt):
            return (pt[b * n_pages + (n_pages - 1 - (c * np_ + i))], 0, 0)
        return f

    page_spec = [pl.BlockSpec((1, PAGE_SIZE, ATTN_WIDTH), page_map(i)) for i in range(np_)]
    grid_spec = pltpu.PrefetchScalarGridSpec(
        num_scalar_prefetch=1,
        grid=(nseq, nchunk),
        in_specs=[pl.BlockSpec((N_HEADS, 1), lambda b, c, pt: (0, 0)),
                  pl.BlockSpec((1, 1, ATTN_WIDTH), lambda b, c, pt: (b, 0, 0))]
                 + page_spec + page_spec,
        out_specs=pl.BlockSpec((1, 1, ATTN_WIDTH), lambda b, c, pt: (b, 0, 0)),
        scratch_shapes=[pltpu.VMEM((N_HEADS, 1), F32), pltpu.VMEM((N_HEADS, ATTN_WIDTH), F32)],
    )
    out = pl.pallas_call(
        _attn_sample_body,
        out_shape=jax.ShapeDtypeStruct((nseq, 1, ATTN_WIDTH), BF16),
        grid_spec=grid_spec,
        compiler_params=_params(("parallel", "arbitrary")),
        name="attn_sample",
    )(page_table.reshape(-1), bias.reshape(N_HEADS, 1), q.reshape(nseq, 1, ATTN_WIDTH),
      *([cache_k] * np_), *([cache_v] * np_))
    return out.reshape(nseq, ATTN_WIDTH)


def _s5_prep_body(lr_ref, li_ref, ldt_ref, br_ref, bi_ref, ar_ref, ai_ref, bbr_ref, bbi_ref):
    lr = lr_ref[...]
    li = li_ref[...]
    dt = jnp.exp(ldt_ref[...])
    mag = jnp.exp(lr * dt)
    ar = mag * jnp.cos(li * dt)
    ai = mag * jnp.sin(li * dt)
    den = lr * lr + li * li
    nr = ar - 1.0
    ni = ai
    fr = (nr * lr + ni * li) / den
    fi = (ni * lr - nr * li) / den
    ar_ref[...] = ar
    ai_ref[...] = ai
    br = br_ref[...]
    bi = bi_ref[...]
    bbr_ref[...] = fr[:, None, :] * br - fi[:, None, :] * bi
    bbi_ref[...] = fr[:, None, :] * bi + fi[:, None, :] * br


def _s5_prep(lam_re, lam_im, log_dt, b_re, b_im):
    g, p, h = N_GROUPS, SSM_STATE, SSM_GROUP
    outs = [jax.ShapeDtypeStruct((g, p), F32)] * 2 + [jax.ShapeDtypeStruct((g, h, p), F32)] * 2
    return pl.pallas_call(_s5_prep_body, out_shape=outs, name="s5_prep")(
        lam_re, lam_im, log_dt.reshape(g, 1),
        jnp.swapaxes(b_re, 1, 2), jnp.swapaxes(b_im, 1, 2))


def _s5_weights(ar, ai, bbr_t, bbi_t, c_re, c_im):
    eye = jnp.eye(8, dtype=F32)

    def drive(bb_t):
        x = bb_t.reshape(S5_CHUNKS, 8, SSM_GROUP, SSM_STATE)
        return (x[:, :, :, None, :] * eye[None, :, None, :, None]).reshape(
            S5_CHUNKS, LANES, 8 * SSM_STATE).astype(BF16)

    def readout(cm):
        x = jnp.swapaxes(cm, 1, 2).reshape(S5_CHUNKS, 8, SSM_STATE, SSM_GROUP)
        return (x[:, :, :, None, :] * eye[None, :, None, :, None]).reshape(
            S5_CHUNKS, 8 * SSM_STATE, LANES).astype(BF16)

    return (ar.reshape(1, SSM_FLAT), ai.reshape(1, SSM_FLAT), drive(bbr_t), drive(bbi_t),
            readout(c_re), readout(-c_im))


def _gelu_tanh(x):
    return 0.5 * x * (1.0 + jnp.tanh(math.sqrt(2.0 / math.pi) * (x + 0.044715 * (x * x * x))))


def _s5_prompt_body(u_ref, ar_ref, ai_ref, bbr_ref, bbi_ref, cr_ref, nci_ref, d_ref,
                    z_ref, hre_ref, him_ref, hs_ref, st_ref, *, nb, tt, cw):
    i = pl.program_id(0)
    cs = 8 * SSM_STATE
    tiles_per_chunk = cs // LANES
    im0 = SSM_FLAT // LANES

    @pl.when(i == 0)
    def _():
        st_ref[...] = jnp.zeros_like(st_ref)

    for b in range(nb):
        rows = pl.ds(b, tt, stride=nb)
        for c in range(S5_CHUNKS):
            uc = u_ref[b, :, c * LANES:(c + 1) * LANES].astype(BF16)
            dre = _dot(uc, bbr_ref[c])
            dim = _dot(uc, bbi_ref[c])
            for m in range(tiles_per_chunk):
                lanes = slice(m * LANES, (m + 1) * LANES)
                hs_ref[c * tiles_per_chunk + m, rows, :] = dre[:, lanes]
                hs_ref[im0 + c * tiles_per_chunk + m, rows, :] = dim[:, lanes]

    nt_scan = cw // LANES
    for j in range(SSM_FLAT // cw):
        cols = slice(j * cw, (j + 1) * cw)
        ar = [jnp.broadcast_to(ar_ref[:, j * cw + m * LANES:j * cw + (m + 1) * LANES], (nb, LANES))
              for m in range(nt_scan)]
        ai = [jnp.broadcast_to(ai_ref[:, j * cw + m * LANES:j * cw + (m + 1) * LANES], (nb, LANES))
              for m in range(nt_scan)]

        def step(t, carry):
            rows = pl.ds(pl.multiple_of(t * nb, nb), nb)
            out = []
            for m in range(nt_scan):
                hr, hi = carry[2 * m], carry[2 * m + 1]
                tre = j * nt_scan + m
                nr = ar[m] * hr - ai[m] * hi + hs_ref[tre, rows, :]
                ni = ar[m] * hi + ai[m] * hr + hs_ref[im0 + tre, rows, :]
                hs_ref[tre, rows, :] = nr
                hs_ref[im0 + tre, rows, :] = ni
                out += [nr, ni]
            return tuple(out)

        init = []
        for m in range(nt_scan):
            lanes = slice(j * cw + m * LANES, j * cw + (m + 1) * LANES)
            init += [st_ref[0, :, lanes], st_ref[1, :, lanes]]
        fin = lax.fori_loop(0, tt, step, tuple(init), unroll=8)
        for m in range(nt_scan):
            lanes = slice(j * cw + m * LANES, j * cw + (m + 1) * LANES)
            st_ref[0, :, lanes] = fin[2 * m]
            st_ref[1, :, lanes] = fin[2 * m + 1]

    for b in range(nb):
        rows = pl.ds(b, tt, stride=nb)
        for c in range(S5_CHUNKS):
            t0 = c * tiles_per_chunk
            hre = jnp.concatenate([hs_ref[t0 + m, rows, :] for m in range(tiles_per_chunk)], axis=-1)
            him = jnp.concatenate([hs_ref[im0 + t0 + m, rows, :] for m in range(tiles_per_chunk)], axis=-1)
            lanes = slice(c * LANES, (c + 1) * LANES)
            y = (_dot(hre.astype(BF16), cr_ref[c]) + _dot(him.astype(BF16), nci_ref[c])
                 + d_ref[:, lanes] * u_ref[b, :, lanes])
            z_ref[b, :, lanes] = _gelu_tanh(y).astype(z_ref.dtype)

    @pl.when(i == pl.num_programs(0) - 1)
    def _():
        hre_ref[...] = st_ref[0]
        him_ref[...] = st_ref[1]


def _s5_prompt(u, weights, d_skip):
    nb, nt, _ = u.shape
    tt = min(S5_TIME_BLOCK, nt)
    ar, ai, bbr, bbi, cr, nci = weights
    full2 = lambda i: (0, 0)
    full3 = lambda i: (0, 0, 0)
    cs = 8 * SSM_STATE
    body = functools.partial(_s5_prompt_body, nb=nb, tt=tt, cw=512)
    return pl.pallas_call(
        body,
        out_shape=[jax.ShapeDtypeStruct((nb, nt, SSM_WIDTH), BF16),
                   jax.ShapeDtypeStruct((nb, SSM_FLAT), F32),
                   jax.ShapeDtypeStruct((nb, SSM_FLAT), F32)],
        grid=(nt // tt,),
        in_specs=[pl.BlockSpec((nb, tt, SSM_WIDTH), lambda i: (0, i, 0)),
                  pl.BlockSpec((1, SSM_FLAT), full2), pl.BlockSpec((1, SSM_FLAT), full2),
                  pl.BlockSpec((S5_CHUNKS, LANES, cs), full3), pl.BlockSpec((S5_CHUNKS, LANES, cs), full3),
                  pl.BlockSpec((S5_CHUNKS, cs, LANES), full3), pl.BlockSpec((S5_CHUNKS, cs, LANES), full3),
                  pl.BlockSpec((1, SSM_WIDTH), full2)],
        out_specs=[pl.BlockSpec((nb, tt, SSM_WIDTH), lambda i: (0, i, 0)),
                   pl.BlockSpec((nb, SSM_FLAT), full2), pl.BlockSpec((nb, SSM_FLAT), full2)],
        scratch_shapes=[pltpu.VMEM((2 * SSM_FLAT // LANES, tt * nb, LANES), F32),
                        pltpu.VMEM((2, nb, SSM_FLAT), F32)],
        compiler_params=_params(("arbitrary",)),
        name="s5_prompt",
    )(u, ar, ai, bbr, bbi, cr, nci, d_skip.reshape(1, SSM_WIDTH))


def _s5_step_body(u_ref, h0r_ref, h0i_ref, ar_ref, ai_ref, bbr_ref, bbi_ref, cr_ref, nci_ref, d_ref,
                  z_ref, hre_ref, him_ref):
    cs = 8 * SSM_STATE
    u = u_ref[...]
    for c in range(S5_CHUNKS):
        cols = slice(c * cs, (c + 1) * cs)
        lanes = slice(c * LANES, (c + 1) * LANES)
        uc = u[:, lanes].astype(BF16)
        ar = ar_ref[:, cols]
        ai = ai_ref[:, cols]
        h0r = h0r_ref[:, cols]
        h0i = h0i_ref[:, cols]
        hr = ar * h0r - ai * h0i + _dot(uc, bbr_ref[c])
        hi = ar * h0i + ai * h0r + _dot(uc, bbi_ref[c])
        hre_ref[:, cols] = hr
        him_ref[:, cols] = hi
        y = _dot(hr.astype(BF16), cr_ref[c]) + _dot(hi.astype(BF16), nci_ref[c]) + d_ref[:, lanes] * u[:, lanes]
        z_ref[:, lanes] = _gelu_tanh(y).astype(z_ref.dtype)


def _s5_step(u, h0_re, h0_im, weights, d_skip):
    n = u.shape[0]
    ar, ai, bbr, bbi, cr, nci = weights
    return pl.pallas_call(
        _s5_step_body,
        out_shape=[jax.ShapeDtypeStruct((n, SSM_WIDTH), BF16),
                   jax.ShapeDtypeStruct((n, SSM_FLAT), F32),
                   jax.ShapeDtypeStruct((n, SSM_FLAT), F32)],
        name="s5_step",
    )(u, h0_re, h0_im, ar, ai, bbr, bbi, cr, nci, d_skip.reshape(1, SSM_WIDTH))


def _mix_body(attn_ref, z_ref, g_ref, x_ref, wup_ref, w1_ref, b1_ref, w2_ref, b2_ref, wout_ref,
              nf_ref, rw_ref, rb_ref, *rest, n_blocks):
    h_ref, xn_ref, lg_ref = rest[-3:]

    @pl.when(pl.program_id(0) >= n_blocks)
    def _():
        h_ref[...] = jnp.zeros_like(h_ref)
        xn_ref[...] = jnp.zeros_like(xn_ref)
        lg_ref[...] = jnp.zeros_like(lg_ref)

    pl.when(pl.program_id(0) < n_blocks)(functools.partial(
        _mix_rows, attn_ref, z_ref, g_ref, x_ref, wup_ref, w1_ref, b1_ref, w2_ref, b2_ref, wout_ref,
        nf_ref, rw_ref, rb_ref, h_ref, xn_ref, lg_ref))


def _mix_rows(attn_ref, z_ref, g_ref, x_ref, wup_ref, w1_ref, b1_ref, w2_ref, b2_ref, wout_ref,
              nf_ref, rw_ref, rb_ref, h_ref, xn_ref, lg_ref):
    y_attn = _dot(attn_ref[...], wup_ref[...])
    z = z_ref[...]
    y_ssm = (_dot(z, w1_ref[...]) + b1_ref[...]) * jax.nn.sigmoid(_dot(z, w2_ref[...]) + b2_ref[...])
    g = g_ref[...]
    merged = jax.nn.sigmoid(g[:, :D_MODEL]) * y_attn + jax.nn.sigmoid(g[:, D_MODEL:]) * y_ssm
    h = x_ref[...] + _dot(merged.astype(BF16), wout_ref[...])
    h_ref[...] = h
    ms = jnp.mean(h * h, axis=-1, keepdims=True)
    xn = h * lax.rsqrt(ms + RMS_EPS) * nf_ref[...]
    xn_ref[...] = xn
    lg_ref[...] = _dot(xn.astype(BF16), rw_ref[...]) + rb_ref[...]


def _mix(attn, z, g, x, w, n_total, first_block, prev=None):
    n = x.shape[0]
    tm = ROW_BLOCK
    n_blocks = n // tm
    n_steps = n_blocks if prev is not None else n_total // tm - first_block
    row = lambda i: (jnp.minimum(i, n_blocks - 1), 0)
    orow = lambda i: (i + first_block, 0)
    full = lambda i: (0, 0)
    wspecs = [pl.BlockSpec(a.shape, full) for a in w]
    in_specs = [pl.BlockSpec((tm, ATTN_WIDTH), row), pl.BlockSpec((tm, SSM_WIDTH), row),
                pl.BlockSpec((tm, 2 * D_MODEL), row), pl.BlockSpec((tm, D_MODEL), row)] + wspecs
    args = [attn, z, g, x, *w]
    aliases = {}
    if prev is not None:
        base = len(args)
        in_specs = in_specs + [pl.BlockSpec(memory_space=pl.ANY)] * 3
        args = args + list(prev)
        aliases = {base: 0, base + 1: 1, base + 2: 2}
    return pl.pallas_call(
        functools.partial(_mix_body, n_blocks=n_blocks),
        out_shape=[jax.ShapeDtypeStruct((n_total, D_MODEL), F32),
                   jax.ShapeDtypeStruct((n_total, D_MODEL), F32),
                   jax.ShapeDtypeStruct((n_total, LANES), F32)],
        grid=(n_steps,),
        in_specs=in_specs,
        out_specs=[pl.BlockSpec((tm, D_MODEL), orow), pl.BlockSpec((tm, D_MODEL), orow),
                   pl.BlockSpec((tm, LANES), orow)],
        input_output_aliases=aliases,
        compiler_params=_params(("parallel",)),
        name="mix",
    )(*args)


def _route_body(lg_ref, si_ref, sf_ref, cnt_ref, run_ref):
    tb = ROW_BLOCK
    i = pl.program_id(0)

    @pl.when(i == 0)
    def _():
        run_ref[...] = jnp.zeros_like(run_ref)

    lg = lg_ref[...]
    lane = lax.broadcasted_iota(jnp.int32, (tb, LANES), 1).astype(F32)
    work = lg
    tops, hots, idxs = [], [], []
    for _ in range(TOP_K):
        m = jnp.max(work, axis=-1, keepdims=True)
        idx = jnp.min(jnp.where(work == m, lane, float(LANES)), axis=-1, keepdims=True)
        hot = lane == idx
        work = jnp.where(hot, -jnp.inf, work)
        tops.append(m)
        hots.append(hot)
        idxs.append(idx)
    es = [jnp.exp(m - tops[0]) for m in tops]
    den = es[0] + es[1] + es[2] + es[3]
    sel = jnp.zeros((tb, LANES), F32)
    for hot in hots:
        sel = jnp.where(hot, 1.0, sel)
    r = lax.broadcasted_iota(jnp.int32, (tb, tb), 0)
    c = lax.broadcasted_iota(jnp.int32, (tb, tb), 1)
    earlier = (c < r).astype(BF16)
    rank = _dot(earlier, sel.astype(BF16)) + run_ref[...]
    si = jnp.zeros((tb, LANES), F32)
    sf = jnp.zeros((tb, LANES), F32)
    for k in range(TOP_K):
        rk = jnp.sum(jnp.where(hots[k], rank, 0.0), axis=-1, keepdims=True)
        si = jnp.where(lane == float(k), idxs[k], si)
        si = jnp.where(lane == float(TOP_K + k), rk, si)
        sf = jnp.where(lane == float(k), es[k] / den, sf)
    si_ref[...] = si.astype(jnp.int32)
    sf_ref[...] = sf
    run = run_ref[...] + jnp.sum(sel, axis=0, keepdims=True)
    run_ref[...] = run
    cnt_ref[...] = run


def _route(logits):
    n = logits.shape[0]
    tb = ROW_BLOCK
    row = lambda i: (i, 0)
    return pl.pallas_call(
        _route_body,
        out_shape=[jax.ShapeDtypeStruct((n, LANES), jnp.int32),
                   jax.ShapeDtypeStruct((n, LANES), F32),
                   jax.ShapeDtypeStruct((1, LANES), F32)],
        grid=(n // tb,),
        in_specs=[pl.BlockSpec((tb, LANES), row)],
        out_specs=[pl.BlockSpec((tb, LANES), row), pl.BlockSpec((tb, LANES), row),
                   pl.BlockSpec((1, LANES), lambda i: (0, 0))],
        scratch_shapes=[pltpu.VMEM((1, LANES), F32)],
        compiler_params=_params(("arbitrary",)),
        name="route",
    )(logits)


def _dispatch_body(pos_ref, x_ref, xs_in_ref, xs_ref, sem):
    del xs_in_ref
    tb = ROW_BLOCK

    def issue(t, _):
        for k in range(TOP_K):
            p = pos_ref[0, 0, t * TOP_K + k]
            pltpu.make_async_copy(x_ref.at[pl.ds(t, 1)], xs_ref.at[pl.ds(p, 1)], sem).start()
        return 0

    lax.fori_loop(0, tb, issue, 0)
    for _ in range(TOP_K):
        pltpu.make_async_copy(x_ref, xs_ref.at[pl.ds(0, tb)], sem).wait()


def _dispatch(pos, xn, n_sorted):
    n = xn.shape[0]
    tb = ROW_BLOCK
    nblk = n // tb
    xs0 = jnp.zeros((n_sorted, D_MODEL), F32)
    return pl.pallas_call(
        _dispatch_body,
        out_shape=jax.ShapeDtypeStruct((n_sorted, D_MODEL), F32),
        grid=(nblk,),
        in_specs=[pl.BlockSpec((1, 1, tb * TOP_K), lambda i: (i, 0, 0), memory_space=pltpu.SMEM),
                  pl.BlockSpec((tb, D_MODEL), lambda i: (i, 0)),
                  pl.BlockSpec(memory_space=pl.ANY)],
        out_specs=pl.BlockSpec(memory_space=pl.ANY),
        scratch_shapes=[pltpu.SemaphoreType.DMA],
        input_output_aliases={2: 0},
        compiler_params=_params(("arbitrary",)),
        name="dispatch",
    )(pos.reshape(nblk, 1, tb * TOP_K), xn, xs0)


def _combine_body(pos_ref, sf_ref, h_ref, nw_ref, ys_ref, y_ref, buf_ref, sem):
    tb = ROW_BLOCK

    def issue(t, _):
        for k in range(TOP_K):
            p = pos_ref[0, 0, t * TOP_K + k]
            pltpu.make_async_copy(ys_ref.at[pl.ds(p, 1)], buf_ref.at[k, pl.ds(t, 1)], sem).start()
        return 0

    lax.fori_loop(0, tb, issue, 0)
    for k in range(TOP_K):
        pltpu.make_async_copy(ys_ref.at[pl.ds(0, tb)], buf_ref.at[k], sem).wait()
    sf = sf_ref[...]
    out = h_ref[...]
    for k in range(TOP_K):
        out = out + sf[:, k:k + 1] * buf_ref[k]
    ms = jnp.mean(out * out, axis=-1, keepdims=True)
    y_ref[...] = out * lax.rsqrt(ms + RMS_EPS) * nw_ref[...]


def _combine(pos, sf, h, norm_w, ys):
    n = h.shape[0]
    tb = ROW_BLOCK
    nblk = n // tb
    row = lambda i: (i, 0)
    return pl.pallas_call(
        _combine_body,
        out_shape=jax.ShapeDtypeStruct((n, D_MODEL), F32),
        grid=(nblk,),
        in_specs=[pl.BlockSpec((1, 1, tb * TOP_K), lambda i: (i, 0, 0), memory_space=pltpu.SMEM),
                  pl.BlockSpec((tb, LANES), row),
                  pl.BlockSpec((tb, D_MODEL), row),
                  pl.BlockSpec((1, D_MODEL), lambda i: (0, 0)),
                  pl.BlockSpec(memory_space=pl.ANY)],
        out_specs=pl.BlockSpec((tb, D_MODEL), row),
        scratch_shapes=[pltpu.VMEM((TOP_K, tb, D_MODEL), F32), pltpu.SemaphoreType.DMA],
        compiler_params=_params(("arbitrary",)),
        name="combine",
    )(pos.reshape(nblk, 1, tb * TOP_K), sf, h, norm_w, ys)


def _experts_body(te_ref, tv_ref, xs_ref, wg_ref, bg_ref, wl_ref, bl_ref, wd_ref, bd_ref, ys_ref):
    i = pl.program_id(0)

    @pl.when(tv_ref[i] == 0)
    def _():
        ys_ref[...] = jnp.zeros_like(ys_ref)

    @pl.when(tv_ref[i] != 0)
    def _():
        x = xs_ref[...].astype(BF16)
        acc = jnp.zeros((MOE_TILE, D_MODEL), F32)
        for f in range(D_FF // MOE_FF_CHUNK):
            cols = slice(f * MOE_FF_CHUNK, (f + 1) * MOE_FF_CHUNK)
            x_glu = jnp.minimum(_dot(x, wg_ref[0, :, cols]) + bg_ref[0, :, cols], SWIGLU_LIMIT)
            x_lin = jnp.clip(_dot(x, wl_ref[0, :, cols]) + bl_ref[0, :, cols], -SWIGLU_LIMIT, SWIGLU_LIMIT)
            act = x_glu * jax.nn.sigmoid(SWIGLU_ALPHA * x_glu) * (x_lin + 1.0)
            acc = acc + _dot(act.astype(BF16), wd_ref[0, cols, :])
        ys_ref[...] = acc + bd_ref[0]


def _experts(tile_expert, tile_valid, xs, wg, bg, wl, bl, wd, bd):
    n_tiles = tile_expert.shape[0]
    tm = MOE_TILE
    wmap = lambda i, te, tv: (te[i], 0, 0)
    grid_spec = pltpu.PrefetchScalarGridSpec(
        num_scalar_prefetch=2,
        grid=(n_tiles,),
        in_specs=[pl.BlockSpec((tm, D_MODEL), lambda i, te, tv: (i, 0)),
                  pl.BlockSpec((1, D_MODEL, D_FF), wmap), pl.BlockSpec((1, 1, D_FF), wmap),
                  pl.BlockSpec((1, D_MODEL, D_FF), wmap), pl.BlockSpec((1, 1, D_FF), wmap),
                  pl.BlockSpec((1, D_FF, D_MODEL), wmap), pl.BlockSpec((1, 1, D_MODEL), wmap)],
        out_specs=pl.BlockSpec((tm, D_MODEL), lambda i, te, tv: (i, 0)),
    )
    return pl.pallas_call(
        _experts_body,
        out_shape=jax.ShapeDtypeStruct((n_tiles * tm, D_MODEL), F32),
        grid_spec=grid_spec,
        compiler_params=_params(("arbitrary",)),
        name="experts",
    )(tile_expert, tile_valid, xs, wg, bg, wl, bl, wd, bd)


def _moe_plan(si, counts, n_tiles):
    idx = si[:, :TOP_K]
    rank = si[:, TOP_K:2 * TOP_K]
    cnt = counts[0, :N_EXPERTS].astype(jnp.int32)
    tiles = (cnt + MOE_TILE - 1) // MOE_TILE
    ends = jnp.cumsum(tiles)
    offs = (ends - tiles) * MOE_TILE
    pos = offs[idx] + rank
    tile_id = jnp.arange(n_tiles, dtype=jnp.int32)
    tile_expert = jnp.minimum(jnp.searchsorted(ends, tile_id, side="right"), N_EXPERTS - 1).astype(jnp.int32)
    tile_valid = (tile_id < ends[-1]).astype(jnp.int32)
    return pos.astype(jnp.int32), tile_expert, tile_valid


def _forward(x_prompt, x_sample, cache_k, cache_v, state_re, state_im, page_table,
             norm_mix, w_in, sb_bias, w_attn_up, lam_re, lam_im, log_dt, b_re, b_im, c_re, c_im,
             d_skip, glu_w1, glu_b1, glu_w2, glu_b2, w_out, norm_ffn, router_w, router_b,
             moe_w_up, moe_b_up, moe_w_down, moe_b_down, norm_final):
    nb, nt, _ = x_prompt.shape
    ns = x_sample.shape[0]
    n_p = nb * nt
    tb = ROW_BLOCK
    n_tot = n_p + tb
    row2 = lambda a: a.reshape(1, -1)

    w_in_bf = w_in.astype(BF16)
    mix_w = [w_attn_up.astype(BF16), glu_w1.astype(BF16), row2(glu_b1), glu_w2.astype(BF16), row2(glu_b2),
             w_out.astype(BF16), row2(norm_ffn),
             jnp.pad(router_w, ((0, 0), (0, LANES - N_EXPERTS))).astype(BF16),
             jnp.pad(row2(router_b), ((0, 0), (0, LANES - N_EXPERTS)), constant_values=NEG_BIG)]
    wg = moe_w_up[:, :, 0::2].astype(BF16)
    wl = moe_w_up[:, :, 1::2].astype(BF16)
    bg = moe_b_up[:, None, 0::2]
    bl = moe_b_up[:, None, 1::2]
    wd = moe_w_down.astype(BF16)
    bd = moe_b_down[:, None, :]
    ar, ai, bbr_t, bbi_t = _s5_prep(lam_re, lam_im, log_dt, b_re, b_im)
    s5w = _s5_weights(ar, ai, bbr_t, bbi_t, c_re, c_im)

    xp = x_prompt.reshape(n_p, D_MODEL)
    q, k, v, kb, vb, u, g = _inproj(xp, row2(norm_mix), w_in_bf)
    attn = _attn_prompt(q, kb, vb, sb_bias, nb, nt)
    z, hre_p, him_p = _s5_prompt(u.reshape(nb, nt, SSM_WIDTH), s5w, d_skip)
    outs = _mix(attn, z.reshape(n_p, SSM_WIDTH), g, xp, mix_w, n_tot, 0)

    xs_pad = jnp.zeros((tb, D_MODEL), F32).at[:ns].set(x_sample.reshape(ns, D_MODEL))
    q_s, k_s, v_s, _, _, u_s, g_s = _inproj(xs_pad, row2(norm_mix), w_in_bf)
    n_pool = cache_k.shape[0]
    attn_s = _attn_sample(q_s[:ns], cache_k.reshape(n_pool, PAGE_SIZE, ATTN_WIDTH),
                          cache_v.reshape(n_pool, PAGE_SIZE, ATTN_WIDTH), page_table, sb_bias)
    z_s, hre_s, him_s = _s5_step(u_s[:ns], state_re.reshape(ns, SSM_FLAT), state_im.reshape(ns, SSM_FLAT),
                                 s5w, d_skip)
    pad_rows = lambda a: jnp.zeros((tb, a.shape[1]), a.dtype).at[:ns].set(a)
    h, xn, logits = _mix(pad_rows(attn_s), pad_rows(z_s), g_s, xs_pad, mix_w, n_tot, n_p // tb, prev=outs)

    si, sf, counts = _route(logits)
    n_tiles = (n_tot * TOP_K) // MOE_TILE + N_EXPERTS
    pos, tile_expert, tile_valid = _moe_plan(si, counts, n_tiles)
    xs = _dispatch(pos, xn, n_tiles * MOE_TILE)
    ys = _experts(tile_expert, tile_valid, xs, wg, bg, wl, bl, wd, bd)
    y = _combine(pos, sf, h, row2(norm_final), ys)

    heads = (N_HEADS, HEAD_DIM)
    state = (N_GROUPS, SSM_STATE)
    return (y[:n_p].reshape(nb, nt, D_MODEL), y[n_p:n_p + ns].reshape(ns, 1, D_MODEL),
            k.reshape(1, nb, nt, *heads), v.reshape(1, nb, nt, *heads),
            hre_p.reshape(1, nb, *state), him_p.reshape(1, nb, *state),
            k_s[:ns].reshape(1, ns, 1, *heads), v_s[:ns].reshape(1, ns, 1, *heads),
            hre_s.reshape(1, ns, *state), him_s.reshape(1, ns, *state))


def kernel(x_prompt, x_sample, cache_k, cache_v, state_ssm_re, state_ssm_im, page_table, norm_mix, w_in, sb_bias, w_attn_up, ssm_lambda_re, ssm_lambda_im, ssm_log_dt, ssm_b_re, ssm_b_im, ssm_c_re, ssm_c_im, ssm_d, glu_w1, glu_b1, glu_w2, glu_b2, w_out, norm_ffn, router_w, router_b, moe_w_up, moe_b_up, moe_w_down, moe_b_down, norm_final):
    return _forward(x_prompt, x_sample, cache_k[0], cache_v[0], state_ssm_re[0], state_ssm_im[0], page_table,
                    norm_mix[0], w_in[0], sb_bias[0], w_attn_up[0], ssm_lambda_re[0], ssm_lambda_im[0],
                    ssm_log_dt[0], ssm_b_re[0], ssm_b_im[0], ssm_c_re[0], ssm_c_im[0], ssm_d[0],
                    glu_w1[0], glu_b1[0], glu_w2[0], glu_b2[0], w_out[0], norm_ffn[0], router_w[0],
                    router_b[0], moe_w_up[0], moe_b_up[0], moe_w_down[0], moe_b_down[0], norm_final)
```

```python
import functools
import math

import jax
import jax.numpy as jnp
from jax import lax
from jax.experimental import pallas as pl
from jax.experimental.pallas import tpu as pltpu

F32 = jnp.float32
BF16 = jnp.bfloat16

D_MODEL = 1024
N_HEADS = 8
HEAD_DIM = 64
ATTN_WIDTH = N_HEADS * HEAD_DIM
SSM_WIDTH = 512
SSM_GROUP = 16
N_GROUPS = SSM_WIDTH // SSM_GROUP
SSM_STATE = 64
SSM_FLAT = N_GROUPS * SSM_STATE
N_EXPERTS = 32
TOP_K = 4
D_FF = 1024
SWIGLU_LIMIT = 7.0
SWIGLU_ALPHA = 1.702
RMS_EPS = 1e-5
PAGE_SIZE = 128
IN_WIDTH = 3 * ATTN_WIDTH + SSM_WIDTH + 2 * D_MODEL

LANES = 128
ROW_BLOCK = 256
ATTN_BLOCK = 256
HEADS_PER_STEP = 4
S5_CHUNKS = SSM_WIDTH // LANES
S5_TIME_BLOCK = 128
MOE_TILE = 512
MOE_FF_CHUNK = 256
PAGES_PER_STEP = 8
NEG_BIG = -1e30
VMEM_LIMIT = 56 * 1024 * 1024


def _dot(a, b):
    return jnp.dot(a, b, preferred_element_type=F32)


def _dot_nt(a, b):
    return lax.dot_general(a, b, (((1,), (1,)), ((), ())), preferred_element_type=F32)


def _dot_hp(a, b):
    return jnp.dot(a, b, preferred_element_type=F32, precision=lax.Precision.HIGHEST)


def _dot_nt_hp(a, b):
    return lax.dot_general(a, b, (((1,), (1,)), ((), ())), preferred_element_type=F32,
                           precision=lax.Precision.HIGHEST)


def _softplus(z):
    return jnp.maximum(z, 0.0) + jnp.log(1.0 + jnp.exp(-jnp.abs(z)))


def _split_bf16(x):
    hi = x.astype(BF16)
    lo = (x - hi.astype(F32)).astype(BF16)
    return hi, lo


def _params(sem, vmem=VMEM_LIMIT):
    return pltpu.CompilerParams(dimension_semantics=sem, vmem_limit_bytes=vmem)


def _inproj_body(x_ref, nw_ref, wq_ref, wkv_ref, wug_ref, q_ref, kt_ref, vt_ref, ktb_ref, vtb_ref, u_ref, g_ref,
                 *, precise):
    dot, dot_nt = (_dot_hp, _dot_nt_hp) if precise else (_dot, _dot_nt)
    x = x_ref[...]
    ms = jnp.mean(x * x, axis=-1, keepdims=True)
    xn = x * lax.rsqrt(ms + RMS_EPS) * nw_ref[...]
    if not precise:
        xn = xn.astype(BF16)
    a = ATTN_WIDTH
    q_ref[...] = (dot(xn, wq_ref[...]) * (HEAD_DIM ** -0.5)).astype(BF16)
    kvt = dot_nt(wkv_ref[...], xn)
    kt_ref[0] = kvt[:a]
    ktb_ref[0] = kvt[:a].astype(BF16)
    vt_ref[0] = kvt[a:]
    vtb_ref[0] = kvt[a:].astype(BF16)
    ug = dot(xn, wug_ref[...])
    u_ref[...] = ug[:, :SSM_WIDTH]
    g_ref[...] = ug[:, SSM_WIDTH:]


def _inproj(x2d, norm_w, w_in_bf, nb):
    n = x2d.shape[0]
    nt = n // nb
    tm = ROW_BLOCK
    nq = nt // tm
    a = ATTN_WIDTH
    row = lambda b, i: (b * nq + i, 0)
    full = lambda b, i: (0, 0)
    tmaj = lambda b, i: (b, 0, i)
    wq = w_in_bf[:, :a]
    wkv_t = w_in_bf[:, a:3 * a].T
    wug = w_in_bf[:, 3 * a:]
    outs = [
        jax.ShapeDtypeStruct((n, a), BF16),
        jax.ShapeDtypeStruct((nb, a, nt), F32),
        jax.ShapeDtypeStruct((nb, a, nt), F32),
        jax.ShapeDtypeStruct((nb, a, nt), BF16),
        jax.ShapeDtypeStruct((nb, a, nt), BF16),
        jax.ShapeDtypeStruct((n, SSM_WIDTH), F32),
        jax.ShapeDtypeStruct((n, 2 * D_MODEL), F32),
    ]
    return pl.pallas_call(
        functools.partial(_inproj_body, precise=w_in_bf.dtype == F32),
        out_shape=outs,
        grid=(nb, nq),
        in_specs=[pl.BlockSpec((tm, D_MODEL), row),
                  pl.BlockSpec((1, D_MODEL), full),
                  pl.BlockSpec(wq.shape, full), pl.BlockSpec(wkv_t.shape, full), pl.BlockSpec(wug.shape, full)],
        out_specs=[pl.BlockSpec((tm, a), row)] + [pl.BlockSpec((1, a, tm), tmaj)] * 4
                  + [pl.BlockSpec((tm, SSM_WIDTH), row), pl.BlockSpec((tm, 2 * D_MODEL), row)],
        compiler_params=_params(("parallel", "parallel")),
        name="inproj",
    )(x2d, norm_w, wq, wkv_t, wug)


def _attn_body(bias_ref, q_ref, k_ref, v_ref, o_ref, acc_ref):
    tq = tk = ATTN_BLOCK
    hg = pl.program_id(1)
    qi = pl.program_id(2)
    nh = HEADS_PER_STEP
    r = lax.broadcasted_iota(jnp.int32, (tq, tk), 0)
    c = lax.broadcasted_iota(jnp.int32, (tq, tk), 1)
    later = (r > c).astype(BF16)
    causal = c < r
    rows = [slice(h * HEAD_DIM, (h + 1) * HEAD_DIM) for h in range(nh)]
    qh = [q_ref[:, rows[h]] for h in range(nh)]
    bias = [bias_ref[hg * nh + h] for h in range(nh)]
    acc_ref[...] = jnp.zeros_like(acc_ref)

    def block(kb, carries, diagonal):
        start = pl.multiple_of(kb * tk, tk)
        kblk = k_ref[0, :, pl.ds(start, tk)]
        vblk = v_ref[0, :, pl.ds(start, tk)]
        zs = [_dot(qh[h], kblk[rows[h], :]) + bias[h] for h in range(nh)]
        sps = [_softplus(z) for z in zs]
        if diagonal:
            sps = [jnp.where(causal, sp, 0.0) for sp in sps]
        newers = []
        for sp in sps:
            hi, lo = _split_bf16(sp)
            newers.append(_dot(hi, later) + _dot(lo, later))
        new = []
        for h in range(nh):
            w = jnp.exp(zs[h] - sps[h] - newers[h])
            if diagonal:
                w = jnp.where(causal, w, 0.0)
            pv = _dot_nt(w.astype(BF16), vblk[rows[h], :])
            acc_ref[h] += pv if diagonal else jnp.exp(-carries[h]) * pv
            new.append(carries[h] + newers[h][:, 0:1] + sps[h][:, 0:1])
        return tuple(new)

    zero = tuple(jnp.zeros((tq, 1), F32) for _ in range(nh))
    carries = block(qi, zero, True)
    lax.fori_loop(0, qi, lambda j, cs: block(qi - 1 - j, cs, False), carries)
    for h in range(nh):
        o_ref[:, rows[h]] = acc_ref[h].astype(o_ref.dtype)


def _attn_prompt(q, ktb, vtb, bias, nb, nt):
    tq = ATTN_BLOCK
    nq = nt // tq
    width = HEADS_PER_STEP * HEAD_DIM
    ng = ATTN_WIDTH // width
    return pl.pallas_call(
        _attn_body,
        out_shape=jax.ShapeDtypeStruct((nb * nt, ATTN_WIDTH), BF16),
        grid=(nb, ng, nq),
        in_specs=[pl.BlockSpec(memory_space=pltpu.SMEM),
                  pl.BlockSpec((tq, width), lambda b, g, i: (b * nq + i, g)),
                  pl.BlockSpec((1, width, nt), lambda b, g, i: (b, g, 0)),
                  pl.BlockSpec((1, width, nt), lambda b, g, i: (b, g, 0))],
        out_specs=pl.BlockSpec((tq, width), lambda b, g, i: (b * nq + i, g)),
        scratch_shapes=[pltpu.VMEM((HEADS_PER_STEP, tq, HEAD_DIM), F32)],
        compiler_params=_params(("parallel", "parallel", "arbitrary")),
        name="attn_prompt",
    )(bias, q, ktb, vtb)


def _attn_sample_body(pt_ref, bias_ref, q_ref, *refs):
    np_ = PAGES_PER_STEP
    k_refs = refs[:np_]
    v_refs = refs[np_:2 * np_]
    o_ref, carry_ref, acc_ref = refs[2 * np_:]
    c = pl.program_id(1)

    @pl.when(c == 0)
    def _():
        carry_ref[...] = jnp.zeros_like(carry_ref)
        acc_ref[...] = jnp.zeros_like(acc_ref)

    head_of_lane = lax.broadcasted_iota(jnp.int32, (N_HEADS, ATTN_WIDTH), 1) // HEAD_DIM
    head_of_row = lax.broadcasted_iota(jnp.int32, (N_HEADS, ATTN_WIDTH), 0)
    own = head_of_lane == head_of_row
    qrow = q_ref[0].astype(F32)
    qbd = jnp.where(own, jnp.broadcast_to(qrow, (N_HEADS, ATTN_WIDTH)), 0.0).astype(BF16)
    r = lax.broadcasted_iota(jnp.int32, (PAGE_SIZE, PAGE_SIZE), 0)
    cc = lax.broadcasted_iota(jnp.int32, (PAGE_SIZE, PAGE_SIZE), 1)
    later = (r > cc).astype(BF16)
    bias = bias_ref[...]
    kcat = jnp.concatenate([k_refs[i][0].astype(BF16) for i in range(np_)], axis=1)
    vcat = jnp.concatenate([v_refs[i][0].astype(BF16) for i in range(np_)], axis=1)
    z = _dot(qbd, kcat) + bias
    sp = _softplus(z)
    page = lambda a, i: a[:, i * PAGE_SIZE:(i + 1) * PAGE_SIZE]
    sp_rows = jnp.concatenate([page(sp, i) for i in range(np_)], axis=0)
    hi, lo = _split_bf16(sp_rows)
    newer_rows = _dot(hi, later) + _dot(lo, later)
    carry = carry_ref[...]
    ws = []
    for i in range(np_):
        sp_i = page(sp, i)
        newer_i = newer_rows[i * N_HEADS:(i + 1) * N_HEADS, :]
        ws.append(jnp.exp(page(z, i) - sp_i - newer_i - carry))
        carry = carry + newer_i[:, 0:1] + sp_i[:, 0:1]
    carry_ref[...] = carry
    w = jnp.concatenate(ws, axis=1).astype(BF16)
    acc = acc_ref[...] + _dot_nt(w, vcat)
    acc_ref[...] = acc

    @pl.when(c == pl.num_programs(1) - 1)
    def _():
        o_ref[0] = jnp.sum(jnp.where(own, acc, 0.0), axis=0, keepdims=True).astype(o_ref.dtype)


def _attn_sample(q, cache_k, cache_v, page_table, bias):
    nseq, n_pages = page_table.shape
    np_ = PAGES_PER_STEP
    nchunk = n_pages // np_

    def page_map(i):
        def f(b, c, pt):
            return (pt[b * n_pages + (n_pages - 1 - (c * np_ + i))], 0, 0)
        return f

    page_spec = [pl.BlockSpec((1, ATTN_WIDTH, PAGE_SIZE), page_map(i)) for i in range(np_)]
    grid_spec = pltpu.PrefetchScalarGridSpec(
        num_scalar_prefetch=1,
        grid=(nseq, nchunk),
        in_specs=[pl.BlockSpec((N_HEADS, 1), lambda b, c, pt: (0, 0)),
                  pl.BlockSpec((1, 1, ATTN_WIDTH), lambda b, c, pt: (b, 0, 0))]
                 + page_spec + page_spec,
        out_specs=pl.BlockSpec((1, 1, ATTN_WIDTH), lambda b, c, pt: (b, 0, 0)),
        scratch_shapes=[pltpu.VMEM((N_HEADS, 1), F32), pltpu.VMEM((N_HEADS, ATTN_WIDTH), F32)],
    )
    out = pl.pallas_call(
        _attn_sample_body,
        out_shape=jax.ShapeDtypeStruct((nseq, 1, ATTN_WIDTH), BF16),
        grid_spec=grid_spec,
        compiler_params=_params(("parallel", "arbitrary")),
        name="attn_sample",
    )(page_table.reshape(-1), bias.reshape(N_HEADS, 1), q.reshape(nseq, 1, ATTN_WIDTH),
      *([cache_k] * np_), *([cache_v] * np_))
    return out.reshape(nseq, ATTN_WIDTH)


def _s5_prep_body(lr_ref, li_ref, ldt_ref, br_ref, bi_ref, ar_ref, ai_ref, bbr_ref, bbi_ref):
    lr = lr_ref[...]
    li = li_ref[...]
    dt = jnp.exp(ldt_ref[...])
    mag = jnp.exp(lr * dt)
    ar = mag * jnp.cos(li * dt)
    ai = mag * jnp.sin(li * dt)
    den = lr * lr + li * li
    nr = ar - 1.0
    ni = ai
    fr = (nr * lr + ni * li) / den
    fi = (ni * lr - nr * li) / den
    ar_ref[...] = ar
    ai_ref[...] = ai
    br = br_ref[...]
    bi = bi_ref[...]
    bbr_ref[...] = fr[:, None, :] * br - fi[:, None, :] * bi
    bbi_ref[...] = fr[:, None, :] * bi + fi[:, None, :] * br


def _s5_prep(lam_re, lam_im, log_dt, b_re, b_im):
    g, p, h = N_GROUPS, SSM_STATE, SSM_GROUP
    outs = [jax.ShapeDtypeStruct((g, p), F32)] * 2 + [jax.ShapeDtypeStruct((g, h, p), F32)] * 2
    return pl.pallas_call(_s5_prep_body, out_shape=outs, name="s5_prep")(
        lam_re, lam_im, log_dt.reshape(g, 1),
        jnp.swapaxes(b_re, 1, 2), jnp.swapaxes(b_im, 1, 2))


def _s5_weights(ar, ai, bbr_t, bbi_t, c_re, c_im, dtype):
    eye = jnp.eye(8, dtype=F32)

    def drive(bb_t):
        x = bb_t.reshape(S5_CHUNKS, 8, SSM_GROUP, SSM_STATE)
        return (x[:, :, :, None, :] * eye[None, :, None, :, None]).reshape(
            S5_CHUNKS, LANES, 8 * SSM_STATE).astype(dtype)

    def readout(cm):
        x = jnp.swapaxes(cm, 1, 2).reshape(S5_CHUNKS, 8, SSM_STATE, SSM_GROUP)
        return (x[:, :, :, None, :] * eye[None, :, None, :, None]).reshape(
            S5_CHUNKS, 8 * SSM_STATE, LANES).astype(dtype)

    return (ar.reshape(1, SSM_FLAT), ai.reshape(1, SSM_FLAT), drive(bbr_t), drive(bbi_t),
            readout(c_re), readout(-c_im))


def _gelu_tanh(x):
    return 0.5 * x * (1.0 + jnp.tanh(math.sqrt(2.0 / math.pi) * (x + 0.044715 * (x * x * x))))


def _s5_prompt_body(u_ref, ar_ref, ai_ref, bbr_ref, bbi_ref, cr_ref, nci_ref, d_ref,
                    z_ref, hre_ref, him_ref, hs_ref, st_ref, *, nb, tt, cw):
    i = pl.program_id(0)
    cs = 8 * SSM_STATE
    tiles_per_chunk = cs // LANES
    im0 = SSM_FLAT // LANES

    @pl.when(i == 0)
    def _():
        st_ref[...] = jnp.zeros_like(st_ref)

    for b in range(nb):
        rows = pl.ds(b, tt, stride=nb)
        for c in range(S5_CHUNKS):
            uc = u_ref[b, :, c * LANES:(c + 1) * LANES].astype(BF16)
            dre = _dot(uc, bbr_ref[c])
            dim = _dot(uc, bbi_ref[c])
            for m in range(tiles_per_chunk):
                lanes = slice(m * LANES, (m + 1) * LANES)
                hs_ref[c * tiles_per_chunk + m, rows, :] = dre[:, lanes]
                hs_ref[im0 + c * tiles_per_chunk + m, rows, :] = dim[:, lanes]

    nt_scan = cw // LANES
    for j in range(SSM_FLAT // cw):
        cols = slice(j * cw, (j + 1) * cw)
        ar = [jnp.broadcast_to(ar_ref[:, j * cw + m * LANES:j * cw + (m + 1) * LANES], (nb, LANES))
              for m in range(nt_scan)]
        ai = [jnp.broadcast_to(ai_ref[:, j * cw + m * LANES:j * cw + (m + 1) * LANES], (nb, LANES))
              for m in range(nt_scan)]

        def step(t, carry):
            rows = pl.ds(pl.multiple_of(t * nb, nb), nb)
            out = []
            for m in range(nt_scan):
                hr, hi = carry[2 * m], carry[2 * m + 1]
                tre = j * nt_scan + m
                nr = ar[m] * hr - ai[m] * hi + hs_ref[tre, rows, :]
                ni = ar[m] * hi + ai[m] * hr + hs_ref[im0 + tre, rows, :]
                hs_ref[tre, rows, :] = nr
                hs_ref[im0 + tre, rows, :] = ni
                out += [nr, ni]
            return tuple(out)

        init = []
        for m in range(nt_scan):
            lanes = slice(j * cw + m * LANES, j * cw + (m + 1) * LANES)
            init += [st_ref[0, :, lanes], st_ref[1, :, lanes]]
        fin = lax.fori_loop(0, tt, step, tuple(init), unroll=8)
        for m in range(nt_scan):
            lanes = slice(j * cw + m * LANES, j * cw + (m + 1) * LANES)
            st_ref[0, :, lanes] = fin[2 * m]
            st_ref[1, :, lanes] = fin[2 * m + 1]

    for b in range(nb):
        rows = pl.ds(b, tt, stride=nb)
        for c in range(S5_CHUNKS):
            t0 = c * tiles_per_chunk
            hre = jnp.concatenate([hs_ref[t0 + m, rows, :] for m in range(tiles_per_chunk)], axis=-1)
            him = jnp.concatenate([hs_ref[im0 + t0 + m, rows, :] for m in range(tiles_per_chunk)], axis=-1)
            lanes = slice(c * LANES, (c + 1) * LANES)
            y = (_dot(hre.astype(BF16), cr_ref[c]) + _dot(him.astype(BF16), nci_ref[c])
                 + d_ref[:, lanes] * u_ref[b, :, lanes])
            z_ref[b, :, lanes] = _gelu_tanh(y).astype(z_ref.dtype)

    @pl.when(i == pl.num_programs(0) - 1)
    def _():
        hre_ref[...] = st_ref[0]
        him_ref[...] = st_ref[1]


def _s5_prompt(u, weights, d_skip):
    nb, nt, _ = u.shape
    tt = min(S5_TIME_BLOCK, nt)
    ar, ai, bbr, bbi, cr, nci = weights
    full2 = lambda i: (0, 0)
    full3 = lambda i: (0, 0, 0)
    cs = 8 * SSM_STATE
    body = functools.partial(_s5_prompt_body, nb=nb, tt=tt, cw=512)
    return pl.pallas_call(
        body,
        out_shape=[jax.ShapeDtypeStruct((nb, nt, SSM_WIDTH), BF16),
                   jax.ShapeDtypeStruct((nb, SSM_FLAT), F32),
                   jax.ShapeDtypeStruct((nb, SSM_FLAT), F32)],
        grid=(nt // tt,),
        in_specs=[pl.BlockSpec((nb, tt, SSM_WIDTH), lambda i: (0, i, 0)),
                  pl.BlockSpec((1, SSM_FLAT), full2), pl.BlockSpec((1, SSM_FLAT), full2),
                  pl.BlockSpec((S5_CHUNKS, LANES, cs), full3), pl.BlockSpec((S5_CHUNKS, LANES, cs), full3),
                  pl.BlockSpec((S5_CHUNKS, cs, LANES), full3), pl.BlockSpec((S5_CHUNKS, cs, LANES), full3),
                  pl.BlockSpec((1, SSM_WIDTH), full2)],
        out_specs=[pl.BlockSpec((nb, tt, SSM_WIDTH), lambda i: (0, i, 0)),
                   pl.BlockSpec((nb, SSM_FLAT), full2), pl.BlockSpec((nb, SSM_FLAT), full2)],
        scratch_shapes=[pltpu.VMEM((2 * SSM_FLAT // LANES, tt * nb, LANES), F32),
                        pltpu.VMEM((2, nb, SSM_FLAT), F32)],
        compiler_params=_params(("arbitrary",)),
        name="s5_prompt",
    )(u, ar, ai, bbr, bbi, cr, nci, d_skip.reshape(1, SSM_WIDTH))


def _s5_step_body(u_ref, h0r_ref, h0i_ref, ar_ref, ai_ref, bbr_ref, bbi_ref, cr_ref, nci_ref, d_ref,
                  z_ref, hre_ref, him_ref):
    cs = 8 * SSM_STATE
    u = u_ref[...]
    for c in range(S5_CHUNKS):
        cols = slice(c * cs, (c + 1) * cs)
        lanes = slice(c * LANES, (c + 1) * LANES)
        uc = u[:, lanes]
        ar = ar_ref[:, cols]
        ai = ai_ref[:, cols]
        h0r = h0r_ref[:, cols]
        h0i = h0i_ref[:, cols]
        hr = ar * h0r - ai * h0i + _dot_hp(uc, bbr_ref[c])
        hi = ar * h0i + ai * h0r + _dot_hp(uc, bbi_ref[c])
        hre_ref[:, cols] = hr
        him_ref[:, cols] = hi
        y = _dot_hp(hr, cr_ref[c]) + _dot_hp(hi, nci_ref[c]) + d_ref[:, lanes] * uc
        z_ref[:, lanes] = _gelu_tanh(y)


def _s5_step(u, h0_re, h0_im, weights, d_skip):
    n = u.shape[0]
    ar, ai, bbr, bbi, cr, nci = weights
    return pl.pallas_call(
        _s5_step_body,
        out_shape=[jax.ShapeDtypeStruct((n, SSM_WIDTH), F32),
                   jax.ShapeDtypeStruct((n, SSM_FLAT), F32),
                   jax.ShapeDtypeStruct((n, SSM_FLAT), F32)],
        name="s5_step",
    )(u, h0_re, h0_im, ar, ai, bbr, bbi, cr, nci, d_skip.reshape(1, SSM_WIDTH))


def _mix_body(attn_ref, z_ref, g_ref, x_ref, wup_ref, w1_ref, b1_ref, w2_ref, b2_ref, wout_ref,
              nf_ref, rw_ref, rb_ref, *rest, n_blocks, precise):
    h_ref, xn_ref, lg_ref = rest[-3:]

    @pl.when(pl.program_id(0) >= n_blocks)
    def _():
        h_ref[...] = jnp.zeros_like(h_ref)
        xn_ref[...] = jnp.zeros_like(xn_ref)
        lg_ref[...] = jnp.zeros_like(lg_ref)

    pl.when(pl.program_id(0) < n_blocks)(functools.partial(
        _mix_rows, attn_ref, z_ref, g_ref, x_ref, wup_ref, w1_ref, b1_ref, w2_ref, b2_ref, wout_ref,
        nf_ref, rw_ref, rb_ref, h_ref, xn_ref, lg_ref, precise=precise))


def _mix_rows(attn_ref, z_ref, g_ref, x_ref, wup_ref, w1_ref, b1_ref, w2_ref, b2_ref, wout_ref,
              nf_ref, rw_ref, rb_ref, h_ref, xn_ref, lg_ref, *, precise):
    dot = _dot_hp if precise else _dot
    operand = (lambda a: a.astype(F32)) if precise else (lambda a: a.astype(BF16))
    y_attn = dot(operand(attn_ref[...]), wup_ref[...])
    z = operand(z_ref[...])
    y_ssm = (dot(z, w1_ref[...]) + b1_ref[...]) * jax.nn.sigmoid(dot(z, w2_ref[...]) + b2_ref[...])
    g = g_ref[...]
    merged = jax.nn.sigmoid(g[:, :D_MODEL]) * y_attn + jax.nn.sigmoid(g[:, D_MODEL:]) * y_ssm
    h = x_ref[...] + dot(operand(merged), wout_ref[...])
    h_ref[...] = h
    ms = jnp.mean(h * h, axis=-1, keepdims=True)
    xn = h * lax.rsqrt(ms + RMS_EPS) * nf_ref[...]
    xn_ref[...] = xn
    lg_ref[...] = dot(operand(xn), rw_ref[...]) + rb_ref[...]


def _mix(attn, z, g, x, w, n_total, first_block, prev=None):
    n = x.shape[0]
    tm = ROW_BLOCK
    n_blocks = n // tm
    n_steps = n_blocks if prev is not None else n_total // tm - first_block
    row = lambda i: (jnp.minimum(i, n_blocks - 1), 0)
    orow = lambda i: (i + first_block, 0)
    full = lambda i: (0, 0)
    wspecs = [pl.BlockSpec(a.shape, full) for a in w]
    in_specs = [pl.BlockSpec((tm, ATTN_WIDTH), row), pl.BlockSpec((tm, SSM_WIDTH), row),
                pl.BlockSpec((tm, 2 * D_MODEL), row), pl.BlockSpec((tm, D_MODEL), row)] + wspecs
    args = [attn, z, g, x, *w]
    aliases = {}
    if prev is not None:
        base = len(args)
        in_specs = in_specs + [pl.BlockSpec(memory_space=pl.ANY)] * 3
        args = args + list(prev)
        aliases = {base: 0, base + 1: 1, base + 2: 2}
    return pl.pallas_call(
        functools.partial(_mix_body, n_blocks=n_blocks, precise=w[0].dtype == F32),
        out_shape=[jax.ShapeDtypeStruct((n_total, D_MODEL), F32),
                   jax.ShapeDtypeStruct((n_total, D_MODEL), F32),
                   jax.ShapeDtypeStruct((n_total, LANES), F32)],
        grid=(n_steps,),
        in_specs=in_specs,
        out_specs=[pl.BlockSpec((tm, D_MODEL), orow), pl.BlockSpec((tm, D_MODEL), orow),
                   pl.BlockSpec((tm, LANES), orow)],
        input_output_aliases=aliases,
        compiler_params=_params(("parallel",)),
        name="mix",
    )(*args)


def _route_body(lg_ref, si_ref, sf_ref, cnt_ref, run_ref):
    tb = ROW_BLOCK
    i = pl.program_id(0)

    @pl.when(i == 0)
    def _():
        run_ref[...] = jnp.zeros_like(run_ref)

    lg = lg_ref[...]
    lane = lax.broadcasted_iota(jnp.int32, (tb, LANES), 1).astype(F32)
    work = lg
    tops, hots, idxs = [], [], []
    for _ in range(TOP_K):
        m = jnp.max(work, axis=-1, keepdims=True)
        idx = jnp.min(jnp.where(work == m, lane, float(LANES)), axis=-1, keepdims=True)
        hot = lane == idx
        work = jnp.where(hot, -jnp.inf, work)
        tops.append(m)
        hots.append(hot)
        idxs.append(idx)
    es = [jnp.exp(m - tops[0]) for m in tops]
    den = es[0] + es[1] + es[2] + es[3]
    sel = jnp.zeros((tb, LANES), F32)
    for hot in hots:
        sel = jnp.where(hot, 1.0, sel)
    r = lax.broadcasted_iota(jnp.int32, (tb, tb), 0)
    c = lax.broadcasted_iota(jnp.int32, (tb, tb), 1)
    earlier = (c < r).astype(BF16)
    rank = _dot(earlier, sel.astype(BF16)) + run_ref[...]
    si = jnp.zeros((tb, LANES), F32)
    sf = jnp.zeros((tb, LANES), F32)
    for k in range(TOP_K):
        rk = jnp.sum(jnp.where(hots[k], rank, 0.0), axis=-1, keepdims=True)
        si = jnp.where(lane == float(k), idxs[k], si)
        si = jnp.where(lane == float(TOP_K + k), rk, si)
        sf = jnp.where(lane == float(k), es[k] / den, sf)
    si_ref[...] = si.astype(jnp.int32)
    sf_ref[...] = sf
    run = run_ref[...] + jnp.sum(sel, axis=0, keepdims=True)
    run_ref[...] = run
    cnt_ref[...] = run


def _route(logits):
    n = logits.shape[0]
    tb = ROW_BLOCK
    row = lambda i: (i, 0)
    return pl.pallas_call(
        _route_body,
        out_shape=[jax.ShapeDtypeStruct((n, LANES), jnp.int32),
                   jax.ShapeDtypeStruct((n, LANES), F32),
                   jax.ShapeDtypeStruct((1, LANES), F32)],
        grid=(n // tb,),
        in_specs=[pl.BlockSpec((tb, LANES), row)],
        out_specs=[pl.BlockSpec((tb, LANES), row), pl.BlockSpec((tb, LANES), row),
                   pl.BlockSpec((1, LANES), lambda i: (0, 0))],
        scratch_shapes=[pltpu.VMEM((1, LANES), F32)],
        compiler_params=_params(("arbitrary",)),
        name="route",
    )(logits)


def _dispatch_body(pos_ref, x_ref, xs_in_ref, xs_ref, sem):
    del xs_in_ref
    tb = ROW_BLOCK

    def issue(t, _):
        for k in range(TOP_K):
            p = pos_ref[0, 0, t * TOP_K + k]
            pltpu.make_async_copy(x_ref.at[pl.ds(t, 1)], xs_ref.at[pl.ds(p, 1)], sem).start()
        return 0

    lax.fori_loop(0, tb, issue, 0)
    for _ in range(TOP_K):
        pltpu.make_async_copy(x_ref, xs_ref.at[pl.ds(0, tb)], sem).wait()


def _dispatch(pos, xn, n_sorted):
    n = xn.shape[0]
    tb = ROW_BLOCK
    nblk = n // tb
    xs0 = jnp.zeros((n_sorted, D_MODEL), F32)
    return pl.pallas_call(
        _dispatch_body,
        out_shape=jax.ShapeDtypeStruct((n_sorted, D_MODEL), F32),
        grid=(nblk,),
        in_specs=[pl.BlockSpec((1, 1, tb * TOP_K), lambda i: (i, 0, 0), memory_space=pltpu.SMEM),
                  pl.BlockSpec((tb, D_MODEL), lambda i: (i, 0)),
                  pl.BlockSpec(memory_space=pl.ANY)],
        out_specs=pl.BlockSpec(memory_space=pl.ANY),
        scratch_shapes=[pltpu.SemaphoreType.DMA],
        input_output_aliases={2: 0},
        compiler_params=_params(("arbitrary",)),
        name="dispatch",
    )(pos.reshape(nblk, 1, tb * TOP_K), xn, xs0)


def _combine_body(pos_ref, sf_ref, h_ref, nw_ref, ys_ref, yp_ref, ys_out_ref, buf_ref, sem, *, n_prompt_blocks):
    tb = ROW_BLOCK
    i = pl.program_id(0)

    def issue(t, _):
        for k in range(TOP_K):
            p = pos_ref[0, 0, t * TOP_K + k]
            pltpu.make_async_copy(ys_ref.at[pl.ds(p, 1)], buf_ref.at[k, pl.ds(t, 1)], sem).start()
        return 0

    lax.fori_loop(0, tb, issue, 0)
    for k in range(TOP_K):
        pltpu.make_async_copy(ys_ref.at[pl.ds(0, tb)], buf_ref.at[k], sem).wait()
    sf = sf_ref[...]
    out = h_ref[...]
    for k in range(TOP_K):
        out = out + sf[:, k:k + 1] * buf_ref[k]
    ms = jnp.mean(out * out, axis=-1, keepdims=True)
    y = out * lax.rsqrt(ms + RMS_EPS) * nw_ref[...]

    @pl.when(i < n_prompt_blocks)
    def _():
        yp_ref[...] = y

    @pl.when(i >= n_prompt_blocks)
    def _():
        ys_out_ref[...] = y


def _combine(pos, sf, h, norm_w, ys, n_prompt):
    n = h.shape[0]
    tb = ROW_BLOCK
    nblk = n // tb
    npb = n_prompt // tb
    row = lambda i: (i, 0)
    return pl.pallas_call(
        functools.partial(_combine_body, n_prompt_blocks=npb),
        out_shape=[jax.ShapeDtypeStruct((n_prompt, D_MODEL), F32),
                   jax.ShapeDtypeStruct((n - n_prompt, D_MODEL), F32)],
        grid=(nblk,),
        in_specs=[pl.BlockSpec((1, 1, tb * TOP_K), lambda i: (i, 0, 0), memory_space=pltpu.SMEM),
                  pl.BlockSpec((tb, LANES), row),
                  pl.BlockSpec((tb, D_MODEL), row),
                  pl.BlockSpec((1, D_MODEL), lambda i: (0, 0)),
                  pl.BlockSpec(memory_space=pl.ANY)],
        out_specs=[pl.BlockSpec((tb, D_MODEL), lambda i: (jnp.minimum(i, npb - 1), 0)),
                   pl.BlockSpec((tb, D_MODEL), lambda i: (jnp.maximum(i - npb, 0), 0))],
        scratch_shapes=[pltpu.VMEM((TOP_K, tb, D_MODEL), F32), pltpu.SemaphoreType.DMA],
        compiler_params=_params(("arbitrary",)),
        name="combine",
    )(pos.reshape(nblk, 1, tb * TOP_K), sf, h, norm_w, ys)


def _experts_body(te_ref, tv_ref, xs_ref, wu_ref, bu_ref, wd_ref, bd_ref, ys_ref):
    i = pl.program_id(0)

    @pl.when(tv_ref[i] == 0)
    def _():
        ys_ref[...] = jnp.zeros_like(ys_ref)

    @pl.when(tv_ref[i] != 0)
    def _():
        x = xs_ref[...].astype(BF16)
        even = lax.broadcasted_iota(jnp.int32, (MOE_TILE, LANES), 1) % 2 == 0
        acc = jnp.zeros((MOE_TILE, D_MODEL), F32)
        for f in range(D_FF // MOE_FF_CHUNK):
            c1 = slice(f * MOE_FF_CHUNK, (f + 1) * MOE_FF_CHUNK)
            c2 = slice(D_FF + f * MOE_FF_CHUNK, D_FF + (f + 1) * MOE_FF_CHUNK)
            h1 = _dot(x, wu_ref[0, :, c1]) + bu_ref[0, :, c1]
            h2 = _dot(x, wu_ref[0, :, c2]) + bu_ref[0, :, c2]
            glu, lin = [], []
            for m in range(MOE_FF_CHUNK // LANES):
                a = h1[:, m * LANES:(m + 1) * LANES]
                b = h2[:, m * LANES:(m + 1) * LANES]
                glu.append(jnp.where(even, a, pltpu.roll(b, 1, 1)))
                lin.append(jnp.where(even, pltpu.roll(a, LANES - 1, 1), b))
            x_glu = jnp.minimum(jnp.concatenate(glu, axis=-1), SWIGLU_LIMIT)
            x_lin = jnp.clip(jnp.concatenate(lin, axis=-1), -SWIGLU_LIMIT, SWIGLU_LIMIT)
            act = x_glu * jax.nn.sigmoid(SWIGLU_ALPHA * x_glu) * (x_lin + 1.0)
            acc = acc + _dot(act.astype(BF16), wd_ref[0, c1, :])
        ys_ref[...] = acc + bd_ref[0]


def _experts(tile_expert, tile_valid, xs, wu, bu, wd, bd):
    n_tiles = tile_expert.shape[0]
    tm = MOE_TILE
    wmap = lambda i, te, tv: (te[i], 0, 0)
    grid_spec = pltpu.PrefetchScalarGridSpec(
        num_scalar_prefetch=2,
        grid=(n_tiles,),
        in_specs=[pl.BlockSpec((tm, D_MODEL), lambda i, te, tv: (i, 0)),
                  pl.BlockSpec((1, D_MODEL, 2 * D_FF), wmap), pl.BlockSpec((1, 1, 2 * D_FF), wmap),
                  pl.BlockSpec((1, D_FF, D_MODEL), wmap), pl.BlockSpec((1, 1, D_MODEL), wmap)],
        out_specs=pl.BlockSpec((tm, D_MODEL), lambda i, te, tv: (i, 0)),
    )
    return pl.pallas_call(
        _experts_body,
        out_shape=jax.ShapeDtypeStruct((n_tiles * tm, D_MODEL), F32),
        grid_spec=grid_spec,
        compiler_params=_params(("arbitrary",)),
        name="experts",
    )(tile_expert, tile_valid, xs, wu, bu, wd, bd)


def _moe_plan(si, counts, n_tiles):
    idx = si[:, :TOP_K]
    rank = si[:, TOP_K:2 * TOP_K]
    cnt = counts[0, :N_EXPERTS].astype(jnp.int32)
    tiles = (cnt + MOE_TILE - 1) // MOE_TILE
    ends = jnp.cumsum(tiles)
    offs = (ends - tiles) * MOE_TILE
    pos = offs[idx] + rank
    tile_id = jnp.arange(n_tiles, dtype=jnp.int32)
    tile_expert = jnp.minimum(jnp.sum((tile_id[:, None] >= ends[None, :]).astype(jnp.int32), axis=1),
                              N_EXPERTS - 1)
    tile_valid = (tile_id < ends[-1]).astype(jnp.int32)
    return pos.astype(jnp.int32), tile_expert, tile_valid


def _forward(x_prompt, x_sample, cache_k, cache_v, state_re, state_im, page_table,
             norm_mix, w_in, sb_bias, w_attn_up, lam_re, lam_im, log_dt, b_re, b_im, c_re, c_im,
             d_skip, glu_w1, glu_b1, glu_w2, glu_b2, w_out, norm_ffn, router_w, router_b,
             moe_w_up, moe_b_up, moe_w_down, moe_b_down, norm_final):
    nb, nt, _ = x_prompt.shape
    ns = x_sample.shape[0]
    n_p = nb * nt
    tb = ROW_BLOCK
    n_tot = n_p + tb
    row2 = lambda a: a.reshape(1, -1)

    w_in_bf = w_in.astype(BF16)
    mix_w32 = [w_attn_up, glu_w1, row2(glu_b1), glu_w2, row2(glu_b2), w_out, row2(norm_ffn),
               jnp.pad(router_w, ((0, 0), (0, LANES - N_EXPERTS))),
               jnp.pad(row2(router_b), ((0, 0), (0, LANES - N_EXPERTS)), constant_values=NEG_BIG)]
    mix_w = [a.astype(BF16) if a.shape[0] > 1 else a for a in mix_w32]
    wu = moe_w_up.astype(BF16)
    bu = moe_b_up[:, None, :]
    wd = moe_w_down.reshape(N_EXPERTS, 2, D_FF // 2, D_MODEL).transpose(0, 2, 1, 3).reshape(
        N_EXPERTS, D_FF, D_MODEL).astype(BF16)
    bd = moe_b_down[:, None, :]
    ar, ai, bbr_t, bbi_t = _s5_prep(lam_re, lam_im, log_dt, b_re, b_im)
    s5w = _s5_weights(ar, ai, bbr_t, bbi_t, c_re, c_im, BF16)
    s5w32 = _s5_weights(ar, ai, bbr_t, bbi_t, c_re, c_im, F32)

    xp = x_prompt.reshape(n_p, D_MODEL)
    q, kt, vt, ktb, vtb, u, g = _inproj(xp, row2(norm_mix), w_in_bf, nb)
    attn = _attn_prompt(q, ktb, vtb, sb_bias, nb, nt)
    z, hre_p, him_p = _s5_prompt(u.reshape(nb, nt, SSM_WIDTH), s5w, d_skip)
    outs = _mix(attn, z.reshape(n_p, SSM_WIDTH), g, xp, mix_w, n_tot, 0)

    xs_pad = jnp.zeros((tb, D_MODEL), F32).at[:ns].set(x_sample.reshape(ns, D_MODEL))
    q_s, kt_s, vt_s, _, _, u_s, g_s = _inproj(xs_pad, row2(norm_mix), w_in, 1)
    n_pool = cache_k.shape[0]
    pages = lambda c: c.transpose(0, 2, 3, 1).reshape(n_pool, ATTN_WIDTH, PAGE_SIZE)
    attn_s = _attn_sample(q_s[:ns], pages(cache_k), pages(cache_v), page_table, sb_bias)
    z_s, hre_s, him_s = _s5_step(u_s[:ns], state_re.reshape(ns, SSM_FLAT), state_im.reshape(ns, SSM_FLAT),
                                 s5w32, d_skip)
    pad_rows = lambda a: jnp.zeros((tb, a.shape[1]), a.dtype).at[:ns].set(a)
    h, xn, logits = _mix(pad_rows(attn_s), pad_rows(z_s), g_s, xs_pad, mix_w32, n_tot, n_p // tb, prev=outs)

    si, sf, counts = _route(logits)
    n_tiles = (n_tot * TOP_K) // MOE_TILE + N_EXPERTS
    pos, tile_expert, tile_valid = _moe_plan(si, counts, n_tiles)
    xs = _dispatch(pos, xn, n_tiles * MOE_TILE)
    ys = _experts(tile_expert, tile_valid, xs, wu, bu, wd, bd)
    y_p, y_s = _combine(pos, sf, h, row2(norm_final), ys, n_p)

    heads = (N_HEADS, HEAD_DIM)
    state = (N_GROUPS, SSM_STATE)
    time_major = lambda a, b_, t_: a.reshape(b_, N_HEADS, HEAD_DIM, t_).transpose(0, 3, 1, 2)[None]
    return (y_p.reshape(nb, nt, D_MODEL), y_s[:ns].reshape(ns, 1, D_MODEL),
            time_major(kt, nb, nt), time_major(vt, nb, nt),
            hre_p.reshape(1, nb, *state), him_p.reshape(1, nb, *state),
            time_major(kt_s[:, :, :ns], 1, ns).reshape(1, ns, 1, *heads),
            time_major(vt_s[:, :, :ns], 1, ns).reshape(1, ns, 1, *heads),
            hre_s.reshape(1, ns, *state), him_s.reshape(1, ns, *state))


def kernel(x_prompt, x_sample, cache_k, cache_v, state_ssm_re, state_ssm_im, page_table, norm_mix, w_in, sb_bias, w_attn_up, ssm_lambda_re, ssm_lambda_im, ssm_log_dt, ssm_b_re, ssm_b_im, ssm_c_re, ssm_c_im, ssm_d, glu_w1, glu_b1, glu_w2, glu_b2, w_out, norm_ffn, router_w, router_b, moe_w_up, moe_b_up, moe_w_down, moe_b_down, norm_final):
    return _forward(x_prompt, x_sample, cache_k[0], cache_v[0], state_ssm_re[0], state_ssm_im[0], page_table,
                    norm_mix[0], w_in[0], sb_bias[0], w_attn_up[0], ssm_lambda_re[0], ssm_lambda_im[0],
                    ssm_log_dt[0], ssm_b_re[0], ssm_b_im[0], ssm_c_re[0], ssm_c_im[0], ssm_d[0],
                    glu_w1[0], glu_b1[0], glu_w2[0], glu_b2[0], w_out[0], norm_ffn[0], router_w[0],
                    router_b[0], moe_w_up[0], moe_b_up[0], moe_w_down[0], moe_b_down[0], norm_final)
```

```python
import functools
import math

import jax
import jax.numpy as jnp
from jax import lax
from jax.experimental import pallas as pl
from jax.experimental.pallas import tpu as pltpu

F32 = jnp.float32
BF16 = jnp.bfloat16

D_MODEL = 1024
N_HEADS = 8
HEAD_DIM = 64
ATTN_WIDTH = N_HEADS * HEAD_DIM
SSM_WIDTH = 512
SSM_GROUP = 16
N_GROUPS = SSM_WIDTH // SSM_GROUP
SSM_STATE = 64
SSM_FLAT = N_GROUPS * SSM_STATE
N_EXPERTS = 32
TOP_K = 4
D_FF = 1024
SWIGLU_LIMIT = 7.0
SWIGLU_ALPHA = 1.702
RMS_EPS = 1e-5
PAGE_SIZE = 128
IN_WIDTH = 3 * ATTN_WIDTH + SSM_WIDTH + 2 * D_MODEL

LANES = 128
ROW_BLOCK = 256
ATTN_BLOCK = 256
HEADS_PER_STEP = 4
S5_CHUNKS = SSM_WIDTH // LANES
S5_TIME_BLOCK = 128
MOE_TILE = 512
MOE_FF_CHUNK = 256
PAGES_PER_STEP = 8
NEG_BIG = -1e30
VMEM_LIMIT = 56 * 1024 * 1024


def _dot(a, b):
    return jnp.dot(a, b, preferred_element_type=F32)


def _dot_nt(a, b):
    return lax.dot_general(a, b, (((1,), (1,)), ((), ())), preferred_element_type=F32)


def _dot_hp(a, b):
    return jnp.dot(a, b, preferred_element_type=F32, precision=lax.Precision.HIGHEST)


def _dot_nt_hp(a, b):
    return lax.dot_general(a, b, (((1,), (1,)), ((), ())), preferred_element_type=F32,
                           precision=lax.Precision.HIGHEST)


def _softplus(z):
    return jnp.maximum(z, 0.0) + jnp.log(1.0 + jnp.exp(-jnp.abs(z)))


def _split_bf16(x):
    hi = x.astype(BF16)
    lo = (x - hi.astype(F32)).astype(BF16)
    return hi, lo


def _params(sem, vmem=VMEM_LIMIT):
    return pltpu.CompilerParams(dimension_semantics=sem, vmem_limit_bytes=vmem)


def _inproj_body(x_ref, nw_ref, wq_ref, wkv_ref, wug_ref, q_ref, kt_ref, vt_ref, ktb_ref, vtb_ref, u_ref, g_ref,
                 *, precise):
    dot, dot_nt = (_dot_hp, _dot_nt_hp) if precise else (_dot, _dot_nt)
    x = x_ref[...]
    ms = jnp.mean(x * x, axis=-1, keepdims=True)
    xn = x * lax.rsqrt(ms + RMS_EPS) * nw_ref[...]
    if not precise:
        xn = xn.astype(BF16)
    a = ATTN_WIDTH
    q_ref[...] = (dot(xn, wq_ref[...]) * (HEAD_DIM ** -0.5)).astype(BF16)
    kvt = dot_nt(wkv_ref[...], xn)
    kt_ref[0] = kvt[:a]
    ktb_ref[0] = kvt[:a].astype(BF16)
    vt_ref[0] = kvt[a:]
    vtb_ref[0] = kvt[a:].astype(BF16)
    ug = dot(xn, wug_ref[...])
    u_ref[...] = ug[:, :SSM_WIDTH]
    g_ref[...] = ug[:, SSM_WIDTH:]


def _inproj(x2d, norm_w, w_in_bf, nb):
    n = x2d.shape[0]
    nt = n // nb
    tm = ROW_BLOCK
    nq = nt // tm
    a = ATTN_WIDTH
    row = lambda b, i: (b * nq + i, 0)
    full = lambda b, i: (0, 0)
    tmaj = lambda b, i: (b, 0, i)
    wq = w_in_bf[:, :a]
    wkv_t = w_in_bf[:, a:3 * a].T
    wug = w_in_bf[:, 3 * a:]
    outs = [
        jax.ShapeDtypeStruct((n, a), BF16),
        jax.ShapeDtypeStruct((nb, a, nt), F32),
        jax.ShapeDtypeStruct((nb, a, nt), F32),
        jax.ShapeDtypeStruct((nb, a, nt), BF16),
        jax.ShapeDtypeStruct((nb, a, nt), BF16),
        jax.ShapeDtypeStruct((n, SSM_WIDTH), F32),
        jax.ShapeDtypeStruct((n, 2 * D_MODEL), F32),
    ]
    return pl.pallas_call(
        functools.partial(_inproj_body, precise=w_in_bf.dtype == F32),
        out_shape=outs,
        grid=(nb, nq),
        in_specs=[pl.BlockSpec((tm, D_MODEL), row),
                  pl.BlockSpec((1, D_MODEL), full),
                  pl.BlockSpec(wq.shape, full), pl.BlockSpec(wkv_t.shape, full), pl.BlockSpec(wug.shape, full)],
        out_specs=[pl.BlockSpec((tm, a), row)] + [pl.BlockSpec((1, a, tm), tmaj)] * 4
                  + [pl.BlockSpec((tm, SSM_WIDTH), row), pl.BlockSpec((tm, 2 * D_MODEL), row)],
        compiler_params=_params(("parallel", "parallel")),
        name="inproj",
    )(x2d, norm_w, wq, wkv_t, wug)


def _attn_body(bias_ref, q_ref, k_ref, v_ref, o_ref, acc_ref):
    tq = tk = ATTN_BLOCK
    hg = pl.program_id(1)
    qi = pl.program_id(2)
    nh = HEADS_PER_STEP
    r = lax.broadcasted_iota(jnp.int32, (tq, tk), 0)
    c = lax.broadcasted_iota(jnp.int32, (tq, tk), 1)
    later = (r > c).astype(BF16)
    causal = c < r
    rows = [slice(h * HEAD_DIM, (h + 1) * HEAD_DIM) for h in range(nh)]
    qh = [q_ref[:, rows[h]] for h in range(nh)]
    bias = [bias_ref[hg * nh + h] for h in range(nh)]
    acc_ref[...] = jnp.zeros_like(acc_ref)

    def block(kb, carries, diagonal):
        start = pl.multiple_of(kb * tk, tk)
        kblk = k_ref[0, :, pl.ds(start, tk)]
        vblk = v_ref[0, :, pl.ds(start, tk)]
        zs = [_dot(qh[h], kblk[rows[h], :]) + bias[h] for h in range(nh)]
        sps = [_softplus(z) for z in zs]
        if diagonal:
            sps = [jnp.where(causal, sp, 0.0) for sp in sps]
        newers = []
        for sp in sps:
            hi, lo = _split_bf16(sp)
            newers.append(_dot(hi, later) + _dot(lo, later))
        new = []
        for h in range(nh):
            w = jnp.exp(zs[h] - sps[h] - newers[h])
            if diagonal:
                w = jnp.where(causal, w, 0.0)
            pv = _dot_nt(w.astype(BF16), vblk[rows[h], :])
            acc_ref[h] += pv if diagonal else jnp.exp(-carries[h]) * pv
            new.append(carries[h] + newers[h][:, 0:1] + sps[h][:, 0:1])
        return tuple(new)

    zero = tuple(jnp.zeros((tq, 1), F32) for _ in range(nh))
    carries = block(qi, zero, True)
    lax.fori_loop(0, qi, lambda j, cs: block(qi - 1 - j, cs, False), carries)
    for h in range(nh):
        o_ref[:, rows[h]] = acc_ref[h].astype(o_ref.dtype)


def _attn_prompt(q, ktb, vtb, bias, nb, nt):
    tq = ATTN_BLOCK
    nq = nt // tq
    width = HEADS_PER_STEP * HEAD_DIM
    ng = ATTN_WIDTH // width
    return pl.pallas_call(
        _attn_body,
        out_shape=jax.ShapeDtypeStruct((nb * nt, ATTN_WIDTH), BF16),
        grid=(nb, ng, nq),
        in_specs=[pl.BlockSpec(memory_space=pltpu.SMEM),
                  pl.BlockSpec((tq, width), lambda b, g, i: (b * nq + i, g)),
                  pl.BlockSpec((1, width, nt), lambda b, g, i: (b, g, 0)),
                  pl.BlockSpec((1, width, nt), lambda b, g, i: (b, g, 0))],
        out_specs=pl.BlockSpec((tq, width), lambda b, g, i: (b * nq + i, g)),
        scratch_shapes=[pltpu.VMEM((HEADS_PER_STEP, tq, HEAD_DIM), F32)],
        compiler_params=_params(("parallel", "parallel", "arbitrary")),
        name="attn_prompt",
    )(bias, q, ktb, vtb)


def _attn_sample_body(pt_ref, bias_ref, q_ref, *refs):
    np_ = PAGES_PER_STEP
    k_refs = refs[:np_]
    v_refs = refs[np_:2 * np_]
    o_ref, carry_ref, acc_ref = refs[2 * np_:]
    c = pl.program_id(1)

    @pl.when(c == 0)
    def _():
        carry_ref[...] = jnp.zeros_like(carry_ref)
        acc_ref[...] = jnp.zeros_like(acc_ref)

    head_of_lane = lax.broadcasted_iota(jnp.int32, (N_HEADS, ATTN_WIDTH), 1) // HEAD_DIM
    head_of_row = lax.broadcasted_iota(jnp.int32, (N_HEADS, ATTN_WIDTH), 0)
    own = head_of_lane == head_of_row
    qrow = q_ref[0].astype(F32)
    qbd = jnp.where(own, jnp.broadcast_to(qrow, (N_HEADS, ATTN_WIDTH)), 0.0).astype(BF16)
    r = lax.broadcasted_iota(jnp.int32, (PAGE_SIZE, PAGE_SIZE), 0)
    cc = lax.broadcasted_iota(jnp.int32, (PAGE_SIZE, PAGE_SIZE), 1)
    later = (r > cc).astype(BF16)
    bias = bias_ref[...]
    kcat = jnp.concatenate([k_refs[i][0].astype(BF16) for i in range(np_)], axis=1)
    vcat = jnp.concatenate([v_refs[i][0].astype(BF16) for i in range(np_)], axis=1)
    z = _dot(qbd, kcat) + bias
    sp = _softplus(z)
    page = lambda a, i: a[:, i * PAGE_SIZE:(i + 1) * PAGE_SIZE]
    sp_rows = jnp.concatenate([page(sp, i) for i in range(np_)], axis=0)
    hi, lo = _split_bf16(sp_rows)
    newer_rows = _dot(hi, later) + _dot(lo, later)
    carry = carry_ref[...]
    ws = []
    for i in range(np_):
        sp_i = page(sp, i)
        newer_i = newer_rows[i * N_HEADS:(i + 1) * N_HEADS, :]
        ws.append(jnp.exp(page(z, i) - sp_i - newer_i - carry))
        carry = carry + newer_i[:, 0:1] + sp_i[:, 0:1]
    carry_ref[...] = carry
    w = jnp.concatenate(ws, axis=1).astype(BF16)
    acc = acc_ref[...] + _dot_nt(w, vcat)
    acc_ref[...] = acc

    @pl.when(c == pl.num_programs(1) - 1)
    def _():
        o_ref[0] = jnp.sum(jnp.where(own, acc, 0.0), axis=0, keepdims=True).astype(o_ref.dtype)


def _attn_sample(q, cache_k, cache_v, page_table, bias):
    nseq, n_pages = page_table.shape
    np_ = PAGES_PER_STEP
    nchunk = n_pages // np_

    def page_map(i):
        def f(b, c, pt):
            return (pt[b * n_pages + (n_pages - 1 - (c * np_ + i))], 0, 0)
        return f

    page_spec = [pl.BlockSpec((1, ATTN_WIDTH, PAGE_SIZE), page_map(i)) for i in range(np_)]
    grid_spec = pltpu.PrefetchScalarGridSpec(
        num_scalar_prefetch=1,
        grid=(nseq, nchunk),
        in_specs=[pl.BlockSpec((N_HEADS, 1), lambda b, c, pt: (0, 0)),
                  pl.BlockSpec((1, 1, ATTN_WIDTH), lambda b, c, pt: (b, 0, 0))]
                 + page_spec + page_spec,
        out_specs=pl.BlockSpec((1, 1, ATTN_WIDTH), lambda b, c, pt: (b, 0, 0)),
        scratch_shapes=[pltpu.VMEM((N_HEADS, 1), F32), pltpu.VMEM((N_HEADS, ATTN_WIDTH), F32)],
    )
    out = pl.pallas_call(
        _attn_sample_body,
        out_shape=jax.ShapeDtypeStruct((nseq, 1, ATTN_WIDTH), BF16),
        grid_spec=grid_spec,
        compiler_params=_params(("parallel", "arbitrary")),
        name="attn_sample",
    )(page_table.reshape(-1), bias.reshape(N_HEADS, 1), q.reshape(nseq, 1, ATTN_WIDTH),
      *([cache_k] * np_), *([cache_v] * np_))
    return out.reshape(nseq, ATTN_WIDTH)


def _s5_prep_body(lr_ref, li_ref, ldt_ref, br_ref, bi_ref, ar_ref, ai_ref, bbr_ref, bbi_ref):
    lr = lr_ref[...]
    li = li_ref[...]
    dt = jnp.exp(ldt_ref[...])
    mag = jnp.exp(lr * dt)
    ar = mag * jnp.cos(li * dt)
    ai = mag * jnp.sin(li * dt)
    den = lr * lr + li * li
    nr = ar - 1.0
    ni = ai
    fr = (nr * lr + ni * li) / den
    fi = (ni * lr - nr * li) / den
    ar_ref[...] = ar
    ai_ref[...] = ai
    br = br_ref[...]
    bi = bi_ref[...]
    bbr_ref[...] = fr[:, None, :] * br - fi[:, None, :] * bi
    bbi_ref[...] = fr[:, None, :] * bi + fi[:, None, :] * br


def _s5_prep(lam_re, lam_im, log_dt, b_re, b_im):
    g, p, h = N_GROUPS, SSM_STATE, SSM_GROUP
    outs = [jax.ShapeDtypeStruct((g, p), F32)] * 2 + [jax.ShapeDtypeStruct((g, h, p), F32)] * 2
    return pl.pallas_call(_s5_prep_body, out_shape=outs, name="s5_prep")(
        lam_re, lam_im, log_dt.reshape(g, 1),
        jnp.swapaxes(b_re, 1, 2), jnp.swapaxes(b_im, 1, 2))


def _s5_weights(ar, ai, bbr_t, bbi_t, c_re, c_im, dtype):
    eye = jnp.eye(8, dtype=F32)

    def drive(bb_t):
        x = bb_t.reshape(S5_CHUNKS, 8, SSM_GROUP, SSM_STATE)
        return (x[:, :, :, None, :] * eye[None, :, None, :, None]).reshape(
            S5_CHUNKS, LANES, 8 * SSM_STATE).astype(dtype)

    def readout(cm):
        x = jnp.swapaxes(cm, 1, 2).reshape(S5_CHUNKS, 8, SSM_STATE, SSM_GROUP)
        return (x[:, :, :, None, :] * eye[None, :, None, :, None]).reshape(
            S5_CHUNKS, 8 * SSM_STATE, LANES).astype(dtype)

    return (ar.reshape(1, SSM_FLAT), ai.reshape(1, SSM_FLAT), drive(bbr_t), drive(bbi_t),
            readout(c_re), readout(-c_im))


def _gelu_tanh(x):
    return 0.5 * x * (1.0 + jnp.tanh(math.sqrt(2.0 / math.pi) * (x + 0.044715 * (x * x * x))))


def _s5_prompt_body(u_ref, ar_ref, ai_ref, bbr_ref, bbi_ref, cr_ref, nci_ref, d_ref,
                    z_ref, hre_ref, him_ref, hs_ref, st_ref, *, nb, tt, cw):
    i = pl.program_id(0)
    cs = 8 * SSM_STATE
    tiles_per_chunk = cs // LANES
    im0 = SSM_FLAT // LANES

    @pl.when(i == 0)
    def _():
        st_ref[...] = jnp.zeros_like(st_ref)

    for b in range(nb):
        rows = pl.ds(b, tt, stride=nb)
        for c in range(S5_CHUNKS):
            uc = u_ref[b, :, c * LANES:(c + 1) * LANES].astype(BF16)
            dre = _dot(uc, bbr_ref[c])
            dim = _dot(uc, bbi_ref[c])
            for m in range(tiles_per_chunk):
                lanes = slice(m * LANES, (m + 1) * LANES)
                hs_ref[c * tiles_per_chunk + m, rows, :] = dre[:, lanes]
                hs_ref[im0 + c * tiles_per_chunk + m, rows, :] = dim[:, lanes]

    nt_scan = cw // LANES
    for j in range(SSM_FLAT // cw):
        cols = slice(j * cw, (j + 1) * cw)
        ar = [jnp.broadcast_to(ar_ref[:, j * cw + m * LANES:j * cw + (m + 1) * LANES], (nb, LANES))
              for m in range(nt_scan)]
        ai = [jnp.broadcast_to(ai_ref[:, j * cw + m * LANES:j * cw + (m + 1) * LANES], (nb, LANES))
              for m in range(nt_scan)]

        def step(t, carry):
            rows = pl.ds(pl.multiple_of(t * nb, nb), nb)
            out = []
            for m in range(nt_scan):
                hr, hi = carry[2 * m], carry[2 * m + 1]
                tre = j * nt_scan + m
                nr = ar[m] * hr - ai[m] * hi + hs_ref[tre, rows, :]
                ni = ar[m] * hi + ai[m] * hr + hs_ref[im0 + tre, rows, :]
                hs_ref[tre, rows, :] = nr
                hs_ref[im0 + tre, rows, :] = ni
                out += [nr, ni]
            return tuple(out)

        init = []
        for m in range(nt_scan):
            lanes = slice(j * cw + m * LANES, j * cw + (m + 1) * LANES)
            init += [st_ref[0, :, lanes], st_ref[1, :, lanes]]
        fin = lax.fori_loop(0, tt, step, tuple(init), unroll=8)
        for m in range(nt_scan):
            lanes = slice(j * cw + m * LANES, j * cw + (m + 1) * LANES)
            st_ref[0, :, lanes] = fin[2 * m]
            st_ref[1, :, lanes] = fin[2 * m + 1]

    for b in range(nb):
        rows = pl.ds(b, tt, stride=nb)
        for c in range(S5_CHUNKS):
            t0 = c * tiles_per_chunk
            hre = jnp.concatenate([hs_ref[t0 + m, rows, :] for m in range(tiles_per_chunk)], axis=-1)
            him = jnp.concatenate([hs_ref[im0 + t0 + m, rows, :] for m in range(tiles_per_chunk)], axis=-1)
            lanes = slice(c * LANES, (c + 1) * LANES)
            y = (_dot(hre.astype(BF16), cr_ref[c]) + _dot(him.astype(BF16), nci_ref[c])
                 + d_ref[:, lanes] * u_ref[b, :, lanes])
            z_ref[b, :, lanes] = _gelu_tanh(y).astype(z_ref.dtype)

    @pl.when(i == pl.num_programs(0) - 1)
    def _():
        hre_ref[...] = st_ref[0]
        him_ref[...] = st_ref[1]


def _s5_prompt(u, weights, d_skip):
    nb, nt, _ = u.shape
    tt = min(S5_TIME_BLOCK, nt)
    ar, ai, bbr, bbi, cr, nci = weights
    full2 = lambda i: (0, 0)
    full3 = lambda i: (0, 0, 0)
    cs = 8 * SSM_STATE
    body = functools.partial(_s5_prompt_body, nb=nb, tt=tt, cw=512)
    return pl.pallas_call(
        body,
        out_shape=[jax.ShapeDtypeStruct((nb, nt, SSM_WIDTH), BF16),
                   jax.ShapeDtypeStruct((nb, SSM_FLAT), F32),
                   jax.ShapeDtypeStruct((nb, SSM_FLAT), F32)],
        grid=(nt // tt,),
        in_specs=[pl.BlockSpec((nb, tt, SSM_WIDTH), lambda i: (0, i, 0)),
                  pl.BlockSpec((1, SSM_FLAT), full2), pl.BlockSpec((1, SSM_FLAT), full2),
                  pl.BlockSpec((S5_CHUNKS, LANES, cs), full3), pl.BlockSpec((S5_CHUNKS, LANES, cs), full3),
                  pl.BlockSpec((S5_CHUNKS, cs, LANES), full3), pl.BlockSpec((S5_CHUNKS, cs, LANES), full3),
                  pl.BlockSpec((1, SSM_WIDTH), full2)],
        out_specs=[pl.BlockSpec((nb, tt, SSM_WIDTH), lambda i: (0, i, 0)),
                   pl.BlockSpec((nb, SSM_FLAT), full2), pl.BlockSpec((nb, SSM_FLAT), full2)],
        scratch_shapes=[pltpu.VMEM((2 * SSM_FLAT // LANES, tt * nb, LANES), F32),
                        pltpu.VMEM((2, nb, SSM_FLAT), F32)],
        compiler_params=_params(("arbitrary",)),
        name="s5_prompt",
    )(u, ar, ai, bbr, bbi, cr, nci, d_skip.reshape(1, SSM_WIDTH))


def _s5_step_body(u_ref, h0r_ref, h0i_ref, ar_ref, ai_ref, bbr_ref, bbi_ref, cr_ref, nci_ref, d_ref,
                  z_ref, hre_ref, him_ref):
    cs = 8 * SSM_STATE
    u = u_ref[...]
    for c in range(S5_CHUNKS):
        cols = slice(c * cs, (c + 1) * cs)
        lanes = slice(c * LANES, (c + 1) * LANES)
        uc = u[:, lanes]
        ar = ar_ref[:, cols]
        ai = ai_ref[:, cols]
        h0r = h0r_ref[:, cols]
        h0i = h0i_ref[:, cols]
        hr = ar * h0r - ai * h0i + _dot_hp(uc, bbr_ref[c])
        hi = ar * h0i + ai * h0r + _dot_hp(uc, bbi_ref[c])
        hre_ref[:, cols] = hr
        him_ref[:, cols] = hi
        y = _dot_hp(hr, cr_ref[c]) + _dot_hp(hi, nci_ref[c]) + d_ref[:, lanes] * uc
        z_ref[:, lanes] = _gelu_tanh(y)


def _s5_step(u, h0_re, h0_im, weights, d_skip):
    n = u.shape[0]
    ar, ai, bbr, bbi, cr, nci = weights
    return pl.pallas_call(
        _s5_step_body,
        out_shape=[jax.ShapeDtypeStruct((n, SSM_WIDTH), F32),
                   jax.ShapeDtypeStruct((n, SSM_FLAT), F32),
                   jax.ShapeDtypeStruct((n, SSM_FLAT), F32)],
        name="s5_step",
    )(u, h0_re, h0_im, ar, ai, bbr, bbi, cr, nci, d_skip.reshape(1, SSM_WIDTH))


def _mix_body(attn_ref, z_ref, g_ref, x_ref, wup_ref, w1_ref, b1_ref, w2_ref, b2_ref, wout_ref,
              nf_ref, rw_ref, rb_ref, *rest, n_blocks, precise):
    h_ref, xn_ref, lg_ref = rest[-3:]

    @pl.when(pl.program_id(0) >= n_blocks)
    def _():
        h_ref[...] = jnp.zeros_like(h_ref)
        xn_ref[...] = jnp.zeros_like(xn_ref)
        lg_ref[...] = jnp.zeros_like(lg_ref)

    pl.when(pl.program_id(0) < n_blocks)(functools.partial(
        _mix_rows, attn_ref, z_ref, g_ref, x_ref, wup_ref, w1_ref, b1_ref, w2_ref, b2_ref, wout_ref,
        nf_ref, rw_ref, rb_ref, h_ref, xn_ref, lg_ref, precise=precise))


def _mix_rows(attn_ref, z_ref, g_ref, x_ref, wup_ref, w1_ref, b1_ref, w2_ref, b2_ref, wout_ref,
              nf_ref, rw_ref, rb_ref, h_ref, xn_ref, lg_ref, *, precise):
    dot = _dot_hp if precise else _dot
    operand = (lambda a: a.astype(F32)) if precise else (lambda a: a.astype(BF16))
    y_attn = dot(operand(attn_ref[...]), wup_ref[...])
    z = operand(z_ref[...])
    y_ssm = (dot(z, w1_ref[...]) + b1_ref[...]) * jax.nn.sigmoid(dot(z, w2_ref[...]) + b2_ref[...])
    g = g_ref[...]
    merged = jax.nn.sigmoid(g[:, :D_MODEL]) * y_attn + jax.nn.sigmoid(g[:, D_MODEL:]) * y_ssm
    h = x_ref[...] + dot(operand(merged), wout_ref[...])
    h_ref[...] = h
    ms = jnp.mean(h * h, axis=-1, keepdims=True)
    xn = h * lax.rsqrt(ms + RMS_EPS) * nf_ref[...]
    xn_ref[...] = xn
    lg_ref[...] = dot(operand(xn), rw_ref[...]) + rb_ref[...]


def _mix(attn, z, g, x, w, n_total, first_block, prev=None):
    n = x.shape[0]
    tm = ROW_BLOCK
    n_blocks = n // tm
    n_steps = n_blocks if prev is not None else n_total // tm - first_block
    row = lambda i: (jnp.minimum(i, n_blocks - 1), 0)
    orow = lambda i: (i + first_block, 0)
    full = lambda i: (0, 0)
    wspecs = [pl.BlockSpec(a.shape, full) for a in w]
    in_specs = [pl.BlockSpec((tm, ATTN_WIDTH), row), pl.BlockSpec((tm, SSM_WIDTH), row),
                pl.BlockSpec((tm, 2 * D_MODEL), row), pl.BlockSpec((tm, D_MODEL), row)] + wspecs
    args = [attn, z, g, x, *w]
    aliases = {}
    if prev is not None:
        base = len(args)
        in_specs = in_specs + [pl.BlockSpec(memory_space=pl.ANY)] * 3
        args = args + list(prev)
        aliases = {base: 0, base + 1: 1, base + 2: 2}
    return pl.pallas_call(
        functools.partial(_mix_body, n_blocks=n_blocks, precise=w[0].dtype == F32),
        out_shape=[jax.ShapeDtypeStruct((n_total, D_MODEL), F32),
                   jax.ShapeDtypeStruct((n_total, D_MODEL), F32),
                   jax.ShapeDtypeStruct((n_total, LANES), F32)],
        grid=(n_steps,),
        in_specs=in_specs,
        out_specs=[pl.BlockSpec((tm, D_MODEL), orow), pl.BlockSpec((tm, D_MODEL), orow),
                   pl.BlockSpec((tm, LANES), orow)],
        input_output_aliases=aliases,
        compiler_params=_params(("parallel",)),
        name="mix",
    )(*args)


def _route_body(lg_ref, sf_ref, blk_ref, cnt_ref, run_ref):
    tb = ROW_BLOCK
    i = pl.program_id(0)

    @pl.when(i == 0)
    def _():
        run_ref[...] = jnp.zeros_like(run_ref)

    lg = lg_ref[...]
    lane = lax.broadcasted_iota(jnp.int32, (tb, LANES), 1).astype(F32)
    work = lg
    tops, hots, idxs = [], [], []
    for _ in range(TOP_K):
        m = jnp.max(work, axis=-1, keepdims=True)
        idx = jnp.min(jnp.where(work == m, lane, float(LANES)), axis=-1, keepdims=True)
        hot = lane == idx
        work = jnp.where(hot, -jnp.inf, work)
        tops.append(m)
        hots.append(hot)
        idxs.append(idx)
    es = [jnp.exp(m - tops[0]) for m in tops]
    den = es[0] + es[1] + es[2] + es[3]
    sel = jnp.zeros((tb, LANES), F32)
    for hot in hots:
        sel = jnp.where(hot, 1.0, sel)
    r = lax.broadcasted_iota(jnp.int32, (tb, tb), 0)
    c = lax.broadcasted_iota(jnp.int32, (tb, tb), 1)
    earlier = (c < r).astype(BF16)
    sel_bf = sel.astype(BF16)
    lrank = _dot(earlier, sel_bf)
    lcnt = jnp.sum(sel, axis=0, keepdims=True)
    below = (lax.broadcasted_iota(jnp.int32, (LANES, LANES), 0)
             < lax.broadcasted_iota(jnp.int32, (LANES, LANES), 1)).astype(BF16)
    loff = _dot(jnp.broadcast_to(lcnt, (16, LANES)).astype(BF16), below)[0:1]
    lpos = lrank + loff
    sf = jnp.zeros((tb, LANES), F32)
    for k in range(TOP_K):
        pk = jnp.sum(jnp.where(hots[k], lpos, 0.0), axis=-1, keepdims=True)
        sf = jnp.where(lane == float(k), es[k] / den, sf)
        sf = jnp.where(lane == float(TOP_K + k), pk, sf)
    sf_ref[...] = sf
    blk_ref[0] = jnp.concatenate([lcnt, run_ref[...]] + [jnp.zeros((1, LANES), F32)] * 6, axis=0)
    run = run_ref[...] + lcnt
    run_ref[...] = run
    cnt_ref[...] = run


def _route(logits):
    n = logits.shape[0]
    tb = ROW_BLOCK
    row = lambda i: (i, 0)
    return pl.pallas_call(
        _route_body,
        out_shape=[jax.ShapeDtypeStruct((n, LANES), F32),
                   jax.ShapeDtypeStruct((n // tb, 8, LANES), F32),
                   jax.ShapeDtypeStruct((1, LANES), F32)],
        grid=(n // tb,),
        in_specs=[pl.BlockSpec((tb, LANES), row)],
        out_specs=[pl.BlockSpec((tb, LANES), row), pl.BlockSpec((1, 8, LANES), lambda i: (i, 0, 0)),
                   pl.BlockSpec((1, LANES), lambda i: (0, 0))],
        scratch_shapes=[pltpu.VMEM((1, LANES), F32)],
        compiler_params=_params(("arbitrary",)),
        name="route",
    )(logits)


ROW_TILE = D_MODEL // LANES
PAIRS = ROW_BLOCK * TOP_K
SEG_BITS = tuple(1 << b for b in range(ROW_BLOCK.bit_length() - 1, -1, -1))
PAD_BITS = tuple(1 << b for b in range(MOE_TILE.bit_length() - 2, -1, -1))


def _segment_copies(length, bits, make_copy):
    for bit in bits:
        @pl.when((length & bit) != 0)
        def _(bit=bit):
            make_copy(length & ~(2 * bit - 1), bit)


def _tile_rows(ref, first_row, n_rows):
    return ref.at[pl.ds(pl.multiple_of(first_row * ROW_TILE, ROW_TILE), n_rows * ROW_TILE)]


def _to_row_tiles(ref, value):
    n = value.shape[0]
    for s in range(ROW_TILE):
        ref[pl.ds(s, n, stride=ROW_TILE), :] = value[:, s * LANES:(s + 1) * LANES]


def _from_row_tiles(ref, n):
    return jnp.concatenate([ref[pl.ds(s, n, stride=ROW_TILE), :] for s in range(ROW_TILE)], axis=1)


def _dispatch_body(info_ref, pad_ref, sf_ref, x_ref, xs_ref, xl_ref, sem):
    tb = ROW_BLOCK

    @pl.when(pl.program_id(0) == 0)
    def _():
        xl_ref[...] = jnp.zeros_like(xl_ref)

        def fill(e, _):
            start = pad_ref[e]

            def copy(off, size):
                cp = pltpu.make_async_copy(_tile_rows(xl_ref, 0, size), _tile_rows(xs_ref, start + off, size), sem)
                cp.start()
                cp.wait()

            _segment_copies(pad_ref[N_EXPERTS + e], PAD_BITS, copy)
            return 0

        lax.fori_loop(0, N_EXPERTS, fill, 0)

        def clear_tile(t, _):
            cp = pltpu.make_async_copy(_tile_rows(xl_ref, 0, MOE_TILE), _tile_rows(xs_ref, t * MOE_TILE, MOE_TILE), sem)
            cp.start()
            cp.wait()
            return 0

        lax.fori_loop(pad_ref[2 * N_EXPERTS], xs_ref.shape[0] // (MOE_TILE * ROW_TILE), clear_tile, 0)

    lpos_t = jnp.transpose(sf_ref[...])
    slot = lax.broadcasted_iota(jnp.int32, (PAIRS, tb), 0).astype(F32)
    onehot = jnp.zeros((PAIRS, tb), F32)
    for k in range(TOP_K):
        onehot = onehot + jnp.where(slot == lpos_t[TOP_K + k:TOP_K + k + 1, :], 1.0, 0.0)
    _to_row_tiles(xl_ref, _dot(onehot.astype(BF16), x_ref[...].astype(BF16)))

    def segment(e, _):
        length, src, dst = info_ref[0, 0, e], info_ref[0, 0, N_EXPERTS + e], info_ref[0, 0, 2 * N_EXPERTS + e]
        _segment_copies(length, SEG_BITS, lambda off, size: pltpu.make_async_copy(
            _tile_rows(xl_ref, src + off, size), _tile_rows(xs_ref, dst + off, size), sem).start())
        return 0

    lax.fori_loop(0, N_EXPERTS, segment, 0)
    pltpu.make_async_copy(xl_ref, _tile_rows(xs_ref, 0, PAIRS), sem).wait()


def _dispatch(info, pad, sf, xn, n_sorted):
    n = xn.shape[0]
    tb = ROW_BLOCK
    row = lambda i: (i, 0)
    return pl.pallas_call(
        _dispatch_body,
        out_shape=jax.ShapeDtypeStruct((n_sorted * ROW_TILE, LANES), F32),
        grid=(n // tb,),
        in_specs=[pl.BlockSpec((1, 1, LANES), lambda i: (i, 0, 0), memory_space=pltpu.SMEM),
                  pl.BlockSpec(memory_space=pltpu.SMEM),
                  pl.BlockSpec((tb, LANES), row),
                  pl.BlockSpec((tb, D_MODEL), row)],
        out_specs=pl.BlockSpec(memory_space=pl.ANY),
        scratch_shapes=[pltpu.VMEM((PAIRS * ROW_TILE, LANES), F32), pltpu.SemaphoreType.DMA],
        compiler_params=_params(("arbitrary",)),
        name="dispatch",
    )(info, pad, sf, xn)


def _combine_body(info_ref, sf_ref, h_ref, nw_ref, ys_ref, yp_ref, ys_out_ref, yl_ref, sem, *, n_prompt_blocks):
    tb = ROW_BLOCK
    i = pl.program_id(0)

    def segment(e, _):
        length, dst, src = info_ref[0, 0, e], info_ref[0, 0, N_EXPERTS + e], info_ref[0, 0, 2 * N_EXPERTS + e]
        _segment_copies(length, SEG_BITS, lambda off, size: pltpu.make_async_copy(
            _tile_rows(ys_ref, src + off, size), _tile_rows(yl_ref, dst + off, size), sem).start())
        return 0

    lax.fori_loop(0, N_EXPERTS, segment, 0)
    sf = sf_ref[...]
    slot = lax.broadcasted_iota(jnp.int32, (tb, PAIRS), 1).astype(F32)
    gates = jnp.zeros((tb, PAIRS), F32)
    for k in range(TOP_K):
        gates = gates + jnp.where(slot == sf[:, TOP_K + k:TOP_K + k + 1], sf[:, k:k + 1], 0.0)
    g_hi, g_lo = _split_bf16(gates)
    pltpu.make_async_copy(_tile_rows(ys_ref, 0, PAIRS), yl_ref, sem).wait()
    y_hi, y_lo = _split_bf16(_from_row_tiles(yl_ref, PAIRS))
    out = h_ref[...] + (_dot(g_hi, y_hi) + _dot(g_lo, y_hi) + _dot(g_hi, y_lo))
    ms = jnp.mean(out * out, axis=-1, keepdims=True)
    y = out * lax.rsqrt(ms + RMS_EPS) * nw_ref[...]

    @pl.when(i < n_prompt_blocks)
    def _():
        yp_ref[...] = y

    @pl.when(i >= n_prompt_blocks)
    def _():
        ys_out_ref[...] = y


def _combine(info, sf, h, norm_w, ys, n_prompt):
    n = h.shape[0]
    tb = ROW_BLOCK
    nblk = n // tb
    npb = n_prompt // tb
    row = lambda i: (i, 0)
    return pl.pallas_call(
        functools.partial(_combine_body, n_prompt_blocks=npb),
        out_shape=[jax.ShapeDtypeStruct((n_prompt, D_MODEL), F32),
                   jax.ShapeDtypeStruct((n - n_prompt, D_MODEL), F32)],
        grid=(nblk,),
        in_specs=[pl.BlockSpec((1, 1, LANES), lambda i: (i, 0, 0), memory_space=pltpu.SMEM),
                  pl.BlockSpec((tb, LANES), row),
                  pl.BlockSpec((tb, D_MODEL), row),
                  pl.BlockSpec((1, D_MODEL), lambda i: (0, 0)),
                  pl.BlockSpec(memory_space=pl.ANY)],
        out_specs=[pl.BlockSpec((tb, D_MODEL), lambda i: (jnp.minimum(i, npb - 1), 0)),
                   pl.BlockSpec((tb, D_MODEL), lambda i: (jnp.maximum(i - npb, 0), 0))],
        scratch_shapes=[pltpu.VMEM((PAIRS * ROW_TILE, LANES), F32), pltpu.SemaphoreType.DMA],
        compiler_params=_params(("arbitrary",)),
        name="combine",
    )(info, sf, h, norm_w, ys)


def _experts_body(te_ref, tv_ref, xs_ref, wu_ref, bu_ref, wd_ref, bd_ref, ys_ref):
    i = pl.program_id(0)

    @pl.when(tv_ref[i] == 0)
    def _():
        ys_ref[...] = jnp.zeros_like(ys_ref)

    @pl.when(tv_ref[i] != 0)
    def _():
        x = _from_row_tiles(xs_ref, MOE_TILE).astype(BF16)
        even = lax.broadcasted_iota(jnp.int32, (MOE_TILE, LANES), 1) % 2 == 0
        acc = jnp.zeros((MOE_TILE, D_MODEL), F32)
        for f in range(D_FF // MOE_FF_CHUNK):
            c1 = slice(f * MOE_FF_CHUNK, (f + 1) * MOE_FF_CHUNK)
            c2 = slice(D_FF + f * MOE_FF_CHUNK, D_FF + (f + 1) * MOE_FF_CHUNK)
            h1 = _dot(x, wu_ref[0, :, c1]) + bu_ref[0, :, c1]
            h2 = _dot(x, wu_ref[0, :, c2]) + bu_ref[0, :, c2]
            glu, lin = [], []
            for m in range(MOE_FF_CHUNK // LANES):
                a = h1[:, m * LANES:(m + 1) * LANES]
                b = h2[:, m * LANES:(m + 1) * LANES]
                glu.append(jnp.where(even, a, pltpu.roll(b, 1, 1)))
                lin.append(jnp.where(even, pltpu.roll(a, LANES - 1, 1), b))
            x_glu = jnp.minimum(jnp.concatenate(glu, axis=-1), SWIGLU_LIMIT)
            x_lin = jnp.clip(jnp.concatenate(lin, axis=-1), -SWIGLU_LIMIT, SWIGLU_LIMIT)
            act = x_glu * jax.nn.sigmoid(SWIGLU_ALPHA * x_glu) * (x_lin + 1.0)
            acc = acc + _dot(act.astype(BF16), wd_ref[0, c1, :])
        _to_row_tiles(ys_ref, acc + bd_ref[0])


def _experts(tile_expert, tile_valid, xs, wu, bu, wd, bd):
    n_tiles = tile_expert.shape[0]
    tm = MOE_TILE
    wmap = lambda i, te, tv: (te[i], 0, 0)
    grid_spec = pltpu.PrefetchScalarGridSpec(
        num_scalar_prefetch=2,
        grid=(n_tiles,),
        in_specs=[pl.BlockSpec((tm * ROW_TILE, LANES), lambda i, te, tv: (i * tv[i], 0)),
                  pl.BlockSpec((1, D_MODEL, 2 * D_FF), wmap), pl.BlockSpec((1, 1, 2 * D_FF), wmap),
                  pl.BlockSpec((1, D_FF, D_MODEL), wmap), pl.BlockSpec((1, 1, D_MODEL), wmap)],
        out_specs=pl.BlockSpec((tm * ROW_TILE, LANES), lambda i, te, tv: (i, 0)),
    )
    return pl.pallas_call(
        _experts_body,
        out_shape=jax.ShapeDtypeStruct((n_tiles * tm * ROW_TILE, LANES), F32),
        grid_spec=grid_spec,
        compiler_params=_params(("arbitrary",)),
        name="experts",
    )(tile_expert, tile_valid, xs, wu, bu, wd, bd)


def _moe_plan(blk, counts, n_tiles):
    as_int = lambda a: a.astype(jnp.int32)
    seg_len = as_int(blk[:, 0, :N_EXPERTS])
    before = as_int(blk[:, 1, :N_EXPERTS])
    cnt = as_int(counts[0, :N_EXPERTS])
    tiles = (cnt + MOE_TILE - 1) // MOE_TILE
    ends = jnp.cumsum(tiles)
    offs = (ends - tiles) * MOE_TILE
    local = jnp.cumsum(seg_len, axis=1) - seg_len
    info = jnp.concatenate([seg_len, local, offs[None, :] + before,
                            jnp.zeros((seg_len.shape[0], LANES - 3 * N_EXPERTS), jnp.int32)], axis=1)
    pad = jnp.concatenate([offs + cnt, tiles * MOE_TILE - cnt, ends[-1:]])
    tile_id = jnp.arange(n_tiles, dtype=jnp.int32)
    tile_expert = jnp.minimum(jnp.sum((tile_id[:, None] >= ends[None, :]).astype(jnp.int32), axis=1),
                              N_EXPERTS - 1)
    tile_valid = (tile_id < ends[-1]).astype(jnp.int32)
    return info[:, None, :], pad, tile_expert, tile_valid


def _forward(x_prompt, x_sample, cache_k, cache_v, state_re, state_im, page_table,
             norm_mix, w_in, sb_bias, w_attn_up, lam_re, lam_im, log_dt, b_re, b_im, c_re, c_im,
             d_skip, glu_w1, glu_b1, glu_w2, glu_b2, w_out, norm_ffn, router_w, router_b,
             moe_w_up, moe_b_up, moe_w_down, moe_b_down, norm_final):
    nb, nt, _ = x_prompt.shape
    ns = x_sample.shape[0]
    n_p = nb * nt
    tb = ROW_BLOCK
    n_tot = n_p + tb
    row2 = lambda a: a.reshape(1, -1)

    w_in_bf = w_in.astype(BF16)
    mix_w32 = [w_attn_up, glu_w1, row2(glu_b1), glu_w2, row2(glu_b2), w_out, row2(norm_ffn),
               jnp.pad(router_w, ((0, 0), (0, LANES - N_EXPERTS))),
               jnp.pad(row2(router_b), ((0, 0), (0, LANES - N_EXPERTS)), constant_values=NEG_BIG)]
    mix_w = [a.astype(BF16) if a.shape[0] > 1 else a for a in mix_w32]
    wu = moe_w_up.astype(BF16)
    bu = moe_b_up[:, None, :]
    wd = moe_w_down.reshape(N_EXPERTS, 2, D_FF // 2, D_MODEL).transpose(0, 2, 1, 3).reshape(
        N_EXPERTS, D_FF, D_MODEL).astype(BF16)
    bd = moe_b_down[:, None, :]
    ar, ai, bbr_t, bbi_t = _s5_prep(lam_re, lam_im, log_dt, b_re, b_im)
    s5w = _s5_weights(ar, ai, bbr_t, bbi_t, c_re, c_im, BF16)
    s5w32 = _s5_weights(ar, ai, bbr_t, bbi_t, c_re, c_im, F32)

    xp = x_prompt.reshape(n_p, D_MODEL)
    q, kt, vt, ktb, vtb, u, g = _inproj(xp, row2(norm_mix), w_in_bf, nb)
    attn = _attn_prompt(q, ktb, vtb, sb_bias, nb, nt)
    z, hre_p, him_p = _s5_prompt(u.reshape(nb, nt, SSM_WIDTH), s5w, d_skip)
    outs = _mix(attn, z.reshape(n_p, SSM_WIDTH), g, xp, mix_w, n_tot, 0)

    xs_pad = jnp.zeros((tb, D_MODEL), F32).at[:ns].set(x_sample.reshape(ns, D_MODEL))
    q_s, kt_s, vt_s, _, _, u_s, g_s = _inproj(xs_pad, row2(norm_mix), w_in, 1)
    n_pool = cache_k.shape[0]
    pages = lambda c: c.transpose(0, 2, 3, 1).reshape(n_pool, ATTN_WIDTH, PAGE_SIZE)
    attn_s = _attn_sample(q_s[:ns], pages(cache_k), pages(cache_v), page_table, sb_bias)
    z_s, hre_s, him_s = _s5_step(u_s[:ns], state_re.reshape(ns, SSM_FLAT), state_im.reshape(ns, SSM_FLAT),
                                 s5w32, d_skip)
    pad_rows = lambda a: jnp.zeros((tb, a.shape[1]), a.dtype).at[:ns].set(a)
    h, xn, logits = _mix(pad_rows(attn_s), pad_rows(z_s), g_s, xs_pad, mix_w32, n_tot, n_p // tb, prev=outs)

    sf, blk, counts = _route(logits)
    n_tiles = (n_tot * TOP_K) // MOE_TILE + N_EXPERTS
    info, pad, tile_expert, tile_valid = _moe_plan(blk, counts, n_tiles)
    xs = _dispatch(info, pad, sf, xn, n_tiles * MOE_TILE)
    ys = _experts(tile_expert, tile_valid, xs, wu, bu, wd, bd)
    y_p, y_s = _combine(info, sf, h, row2(norm_final), ys, n_p)

    heads = (N_HEADS, HEAD_DIM)
    state = (N_GROUPS, SSM_STATE)
    time_major = lambda a, b_, t_: a.reshape(b_, N_HEADS, HEAD_DIM, t_).transpose(0, 3, 1, 2)[None]
    return (y_p.reshape(nb, nt, D_MODEL), y_s[:ns].reshape(ns, 1, D_MODEL),
            time_major(kt, nb, nt), time_major(vt, nb, nt),
            hre_p.reshape(1, nb, *state), him_p.reshape(1, nb, *state),
            time_major(kt_s[:, :, :ns], 1, ns).reshape(1, ns, 1, *heads),
            time_major(vt_s[:, :, :ns], 1, ns).reshape(1, ns, 1, *heads),
            hre_s.reshape(1, ns, *state), him_s.reshape(1, ns, *state))


def kernel(x_prompt, x_sample, cache_k, cache_v, state_ssm_re, state_ssm_im, page_table, norm_mix, w_in, sb_bias, w_attn_up, ssm_lambda_re, ssm_lambda_im, ssm_log_dt, ssm_b_re, ssm_b_im, ssm_c_re, ssm_c_im, ssm_d, glu_w1, glu_b1, glu_w2, glu_b2, w_out, norm_ffn, router_w, router_b, moe_w_up, moe_b_up, moe_w_down, moe_b_down, norm_final):
    return _forward(x_prompt, x_sample, cache_k[0], cache_v[0], state_ssm_re[0], state_ssm_im[0], page_table,
                    norm_mix[0], w_in[0], sb_bias[0], w_attn_up[0], ssm_lambda_re[0], ssm_lambda_im[0],
                    ssm_log_dt[0], ssm_b_re[0], ssm_b_im[0], ssm_c_re[0], ssm_c_im[0], ssm_d[0],
                    glu_w1[0], glu_b1[0], glu_w2[0], glu_b2[0], w_out[0], norm_ffn[0], router_w[0],
                    router_b[0], moe_w_up[0], moe_b_up[0], moe_w_down[0], moe_b_down[0], norm_final)
```

```python
import functools
import math

import jax
import jax.numpy as jnp
from jax import lax
from jax.experimental import pallas as pl
from jax.experimental.pallas import tpu as pltpu

F32 = jnp.float32
BF16 = jnp.bfloat16

D_MODEL = 1024
N_HEADS = 8
HEAD_DIM = 64
ATTN_WIDTH = N_HEADS * HEAD_DIM
SSM_WIDTH = 512
SSM_GROUP = 16
N_GROUPS = SSM_WIDTH // SSM_GROUP
SSM_STATE = 64
SSM_FLAT = N_GROUPS * SSM_STATE
N_EXPERTS = 32
TOP_K = 4
D_FF = 1024
SWIGLU_LIMIT = 7.0
SWIGLU_ALPHA = 1.702
RMS_EPS = 1e-5
PAGE_SIZE = 128
IN_WIDTH = 3 * ATTN_WIDTH + SSM_WIDTH + 2 * D_MODEL

LANES = 128
ROW_BLOCK = 256
ATTN_BLOCK = 256
HEADS_PER_STEP = 4
S5_CHUNKS = SSM_WIDTH // LANES
S5_TIME_BLOCK = 128
MOE_TILE = 512
MOE_FF_CHUNK = 256
PAGES_PER_STEP = 16
NEG_BIG = -1e30
VMEM_LIMIT = 56 * 1024 * 1024


def _dot(a, b):
    return jnp.dot(a, b, preferred_element_type=F32)


def _dot_nt(a, b):
    return lax.dot_general(a, b, (((1,), (1,)), ((), ())), preferred_element_type=F32)


def _dot_hp(a, b):
    return jnp.dot(a, b, preferred_element_type=F32, precision=lax.Precision.HIGHEST)


def _dot_nt_hp(a, b):
    return lax.dot_general(a, b, (((1,), (1,)), ((), ())), preferred_element_type=F32,
                           precision=lax.Precision.HIGHEST)


def _softplus(z):
    return jnp.maximum(z, 0.0) + jnp.log(1.0 + jnp.exp(-jnp.abs(z)))


def _split_bf16(x):
    hi = x.astype(BF16)
    lo = (x - hi.astype(F32)).astype(BF16)
    return hi, lo


def _params(sem, vmem=VMEM_LIMIT):
    return pltpu.CompilerParams(dimension_semantics=sem, vmem_limit_bytes=vmem)


def _inproj_body(x_ref, nw_ref, wq_ref, wkv_ref, wug_ref, q_ref, kt_ref, vt_ref, ktb_ref, vtb_ref, u_ref, g_ref,
                 *, precise):
    dot, dot_nt = (_dot_hp, _dot_nt_hp) if precise else (_dot, _dot_nt)
    x = x_ref[...]
    ms = jnp.mean(x * x, axis=-1, keepdims=True)
    xn = x * lax.rsqrt(ms + RMS_EPS) * nw_ref[...]
    if not precise:
        xn = xn.astype(BF16)
    a = ATTN_WIDTH
    q_ref[...] = (dot(xn, wq_ref[...]) * (HEAD_DIM ** -0.5)).astype(BF16)
    kvt = dot_nt(wkv_ref[...], xn)
    kt_ref[0] = kvt[:a]
    ktb_ref[0] = kvt[:a].astype(BF16)
    vt_ref[0] = kvt[a:]
    vtb_ref[0] = kvt[a:].astype(BF16)
    ug = dot(xn, wug_ref[...])
    u_ref[...] = ug[:, :SSM_WIDTH]
    g_ref[...] = ug[:, SSM_WIDTH:]


def _inproj(x2d, norm_w, w_in_bf, nb):
    n = x2d.shape[0]
    nt = n // nb
    tm = ROW_BLOCK
    nq = nt // tm
    a = ATTN_WIDTH
    row = lambda b, i: (b * nq + i, 0)
    full = lambda b, i: (0, 0)
    tmaj = lambda b, i: (b, 0, i)
    wq = w_in_bf[:, :a]
    wkv_t = w_in_bf[:, a:3 * a].T
    wug = w_in_bf[:, 3 * a:]
    outs = [
        jax.ShapeDtypeStruct((n, a), BF16),
        jax.ShapeDtypeStruct((nb, a, nt), F32),
        jax.ShapeDtypeStruct((nb, a, nt), F32),
        jax.ShapeDtypeStruct((nb, a, nt), BF16),
        jax.ShapeDtypeStruct((nb, a, nt), BF16),
        jax.ShapeDtypeStruct((n, SSM_WIDTH), F32),
        jax.ShapeDtypeStruct((n, 2 * D_MODEL), F32),
    ]
    return pl.pallas_call(
        functools.partial(_inproj_body, precise=w_in_bf.dtype == F32),
        out_shape=outs,
        grid=(nb, nq),
        in_specs=[pl.BlockSpec((tm, D_MODEL), row),
                  pl.BlockSpec((1, D_MODEL), full),
                  pl.BlockSpec(wq.shape, full), pl.BlockSpec(wkv_t.shape, full), pl.BlockSpec(wug.shape, full)],
        out_specs=[pl.BlockSpec((tm, a), row)] + [pl.BlockSpec((1, a, tm), tmaj)] * 4
                  + [pl.BlockSpec((tm, SSM_WIDTH), row), pl.BlockSpec((tm, 2 * D_MODEL), row)],
        compiler_params=_params(("parallel", "parallel")),
        name="inproj",
    )(x2d, norm_w, wq, wkv_t, wug)


def _attn_body(bias_ref, q_ref, k_ref, v_ref, o_ref, acc_ref):
    tq = tk = ATTN_BLOCK
    hg = pl.program_id(1)
    qi = pl.program_id(2)
    nh = HEADS_PER_STEP
    r = lax.broadcasted_iota(jnp.int32, (tq, tk), 0)
    c = lax.broadcasted_iota(jnp.int32, (tq, tk), 1)
    later = (r > c).astype(BF16)
    causal = c < r
    rows = [slice(h * HEAD_DIM, (h + 1) * HEAD_DIM) for h in range(nh)]
    qh = [q_ref[:, rows[h]] for h in range(nh)]
    bias = [bias_ref[hg * nh + h] for h in range(nh)]
    acc_ref[...] = jnp.zeros_like(acc_ref)

    def block(kb, carries, diagonal):
        start = pl.multiple_of(kb * tk, tk)
        kblk = k_ref[0, :, pl.ds(start, tk)]
        vblk = v_ref[0, :, pl.ds(start, tk)]
        zs = [_dot(qh[h], kblk[rows[h], :]) + bias[h] for h in range(nh)]
        sps = [_softplus(z) for z in zs]
        if diagonal:
            sps = [jnp.where(causal, sp, 0.0) for sp in sps]
        newers = []
        for sp in sps:
            hi, lo = _split_bf16(sp)
            newers.append(_dot(hi, later) + _dot(lo, later))
        new = []
        for h in range(nh):
            w = jnp.exp(zs[h] - sps[h] - newers[h])
            if diagonal:
                w = jnp.where(causal, w, 0.0)
            pv = _dot_nt(w.astype(BF16), vblk[rows[h], :])
            acc_ref[h] += pv if diagonal else jnp.exp(-carries[h]) * pv
            new.append(carries[h] + newers[h][:, 0:1] + sps[h][:, 0:1])
        return tuple(new)

    zero = tuple(jnp.zeros((tq, 1), F32) for _ in range(nh))
    carries = block(qi, zero, True)
    lax.fori_loop(0, qi, lambda j, cs: block(qi - 1 - j, cs, False), carries)
    for h in range(nh):
        o_ref[:, rows[h]] = acc_ref[h].astype(o_ref.dtype)


def _attn_prompt(q, ktb, vtb, bias, nb, nt):
    tq = ATTN_BLOCK
    nq = nt // tq
    width = HEADS_PER_STEP * HEAD_DIM
    ng = ATTN_WIDTH // width
    return pl.pallas_call(
        _attn_body,
        out_shape=jax.ShapeDtypeStruct((nb * nt, ATTN_WIDTH), BF16),
        grid=(nb, ng, nq),
        in_specs=[pl.BlockSpec(memory_space=pltpu.SMEM),
                  pl.BlockSpec((tq, width), lambda b, g, i: (b * nq + i, g)),
                  pl.BlockSpec((1, width, nt), lambda b, g, i: (b, g, 0)),
                  pl.BlockSpec((1, width, nt), lambda b, g, i: (b, g, 0))],
        out_specs=pl.BlockSpec((tq, width), lambda b, g, i: (b * nq + i, g)),
        scratch_shapes=[pltpu.VMEM((HEADS_PER_STEP, tq, HEAD_DIM), F32)],
        compiler_params=_params(("parallel", "parallel", "arbitrary")),
        name="attn_prompt",
    )(bias, q, ktb, vtb)


def _attn_sample_body(pt_ref, bias_ref, q_ref, *refs):
    np_ = PAGES_PER_STEP
    k_refs = refs[:np_]
    v_refs = refs[np_:2 * np_]
    o_ref, carry_ref, acc_ref = refs[2 * np_:]
    c = pl.program_id(1)

    @pl.when(c == 0)
    def _():
        carry_ref[...] = jnp.zeros_like(carry_ref)
        acc_ref[...] = jnp.zeros_like(acc_ref)

    head_of_lane = lax.broadcasted_iota(jnp.int32, (N_HEADS, ATTN_WIDTH), 1) // HEAD_DIM
    head_of_row = lax.broadcasted_iota(jnp.int32, (N_HEADS, ATTN_WIDTH), 0)
    own = head_of_lane == head_of_row
    qrow = q_ref[0].astype(F32)
    qbd = jnp.where(own, jnp.broadcast_to(qrow, (N_HEADS, ATTN_WIDTH)), 0.0).astype(BF16)
    r = lax.broadcasted_iota(jnp.int32, (PAGE_SIZE, PAGE_SIZE), 0)
    cc = lax.broadcasted_iota(jnp.int32, (PAGE_SIZE, PAGE_SIZE), 1)
    later = (r > cc).astype(BF16)
    bias = bias_ref[...]
    kcat = jnp.concatenate([k_refs[i][0].astype(BF16) for i in range(np_)], axis=1)
    vcat = jnp.concatenate([v_refs[i][0].astype(BF16) for i in range(np_)], axis=1)
    z = _dot(qbd, kcat) + bias
    sp = _softplus(z)
    page = lambda a, i: a[:, i * PAGE_SIZE:(i + 1) * PAGE_SIZE]
    sp_rows = jnp.concatenate([page(sp, i) for i in range(np_)], axis=0)
    hi, lo = _split_bf16(sp_rows)
    newer_rows = _dot(hi, later) + _dot(lo, later)
    carry = carry_ref[...]
    ws = []
    for i in range(np_):
        sp_i = page(sp, i)
        newer_i = newer_rows[i * N_HEADS:(i + 1) * N_HEADS, :]
        ws.append(jnp.exp(page(z, i) - sp_i - newer_i - carry))
        carry = carry + newer_i[:, 0:1] + sp_i[:, 0:1]
    carry_ref[...] = carry
    w = jnp.concatenate(ws, axis=1).astype(BF16)
    acc = acc_ref[...] + _dot_nt(w, vcat)
    acc_ref[...] = acc

    @pl.when(c == pl.num_programs(1) - 1)
    def _():
        o_ref[0] = jnp.sum(jnp.where(own, acc, 0.0), axis=0, keepdims=True).astype(o_ref.dtype)


def _attn_sample(q, cache_k, cache_v, page_table, bias):
    nseq, n_pages = page_table.shape
    np_ = PAGES_PER_STEP
    nchunk = n_pages // np_

    def page_map(i):
        def f(b, c, pt):
            return (pt[b * n_pages + (n_pages - 1 - (c * np_ + i))], 0, 0)
        return f

    page_spec = [pl.BlockSpec((1, ATTN_WIDTH, PAGE_SIZE), page_map(i)) for i in range(np_)]
    grid_spec = pltpu.PrefetchScalarGridSpec(
        num_scalar_prefetch=1,
        grid=(nseq, nchunk),
        in_specs=[pl.BlockSpec((N_HEADS, 1), lambda b, c, pt: (0, 0)),
                  pl.BlockSpec((1, 1, ATTN_WIDTH), lambda b, c, pt: (b, 0, 0))]
                 + page_spec + page_spec,
        out_specs=pl.BlockSpec((1, 1, ATTN_WIDTH), lambda b, c, pt: (b, 0, 0)),
        scratch_shapes=[pltpu.VMEM((N_HEADS, 1), F32), pltpu.VMEM((N_HEADS, ATTN_WIDTH), F32)],
    )
    out = pl.pallas_call(
        _attn_sample_body,
        out_shape=jax.ShapeDtypeStruct((nseq, 1, ATTN_WIDTH), BF16),
        grid_spec=grid_spec,
        compiler_params=_params(("parallel", "arbitrary")),
        name="attn_sample",
    )(page_table.reshape(-1), bias.reshape(N_HEADS, 1), q.reshape(nseq, 1, ATTN_WIDTH),
      *([cache_k] * np_), *([cache_v] * np_))
    return out.reshape(nseq, ATTN_WIDTH)


def _s5_prep_body(lr_ref, li_ref, ldt_ref, br_ref, bi_ref, ar_ref, ai_ref, bbr_ref, bbi_ref):
    lr = lr_ref[...]
    li = li_ref[...]
    dt = jnp.exp(ldt_ref[...])
    mag = jnp.exp(lr * dt)
    ar = mag * jnp.cos(li * dt)
    ai = mag * jnp.sin(li * dt)
    den = lr * lr + li * li
    nr = ar - 1.0
    ni = ai
    fr = (nr * lr + ni * li) / den
    fi = (ni * lr - nr * li) / den
    ar_ref[...] = ar
    ai_ref[...] = ai
    br = br_ref[...]
    bi = bi_ref[...]
    bbr_ref[...] = fr[:, None, :] * br - fi[:, None, :] * bi
    bbi_ref[...] = fr[:, None, :] * bi + fi[:, None, :] * br


def _s5_prep(lam_re, lam_im, log_dt, b_re, b_im):
    g, p, h = N_GROUPS, SSM_STATE, SSM_GROUP
    outs = [jax.ShapeDtypeStruct((g, p), F32)] * 2 + [jax.ShapeDtypeStruct((g, h, p), F32)] * 2
    return pl.pallas_call(_s5_prep_body, out_shape=outs, name="s5_prep")(
        lam_re, lam_im, log_dt.reshape(g, 1),
        jnp.swapaxes(b_re, 1, 2), jnp.swapaxes(b_im, 1, 2))


def _s5_weights(ar, ai, bbr_t, bbi_t, c_re, c_im, dtype):
    eye = jnp.eye(8, dtype=F32)

    def drive(bb_t):
        x = bb_t.reshape(S5_CHUNKS, 8, SSM_GROUP, SSM_STATE)
        return (x[:, :, :, None, :] * eye[None, :, None, :, None]).reshape(
            S5_CHUNKS, LANES, 8 * SSM_STATE).astype(dtype)

    def readout(cm):
        x = jnp.swapaxes(cm, 1, 2).reshape(S5_CHUNKS, 8, SSM_STATE, SSM_GROUP)
        return (x[:, :, :, None, :] * eye[None, :, None, :, None]).reshape(
            S5_CHUNKS, 8 * SSM_STATE, LANES).astype(dtype)

    return (ar.reshape(1, SSM_FLAT), ai.reshape(1, SSM_FLAT), drive(bbr_t), drive(bbi_t),
            readout(c_re), readout(-c_im))


def _gelu_tanh(x):
    return 0.5 * x * (1.0 + jnp.tanh(math.sqrt(2.0 / math.pi) * (x + 0.044715 * (x * x * x))))


def _s5_prompt_body(u_ref, ar_ref, ai_ref, bbr_ref, bbi_ref, cr_ref, nci_ref, d_ref,
                    z_ref, hre_ref, him_ref, hs_ref, st_ref, *, nb, tt, cw):
    i = pl.program_id(0)
    cs = 8 * SSM_STATE
    tiles_per_chunk = cs // LANES
    im0 = SSM_FLAT // LANES

    @pl.when(i == 0)
    def _():
        st_ref[...] = jnp.zeros_like(st_ref)

    for b in range(nb):
        rows = pl.ds(b, tt, stride=nb)
        for c in range(S5_CHUNKS):
            uc = u_ref[b, :, c * LANES:(c + 1) * LANES].astype(BF16)
            dre = _dot(uc, bbr_ref[c])
            dim = _dot(uc, bbi_ref[c])
            for m in range(tiles_per_chunk):
                lanes = slice(m * LANES, (m + 1) * LANES)
                hs_ref[c * tiles_per_chunk + m, rows, :] = dre[:, lanes]
                hs_ref[im0 + c * tiles_per_chunk + m, rows, :] = dim[:, lanes]

    nt_scan = cw // LANES
    for j in range(SSM_FLAT // cw):
        cols = slice(j * cw, (j + 1) * cw)
        ar = [jnp.broadcast_to(ar_ref[:, j * cw + m * LANES:j * cw + (m + 1) * LANES], (nb, LANES))
              for m in range(nt_scan)]
        ai = [jnp.broadcast_to(ai_ref[:, j * cw + m * LANES:j * cw + (m + 1) * LANES], (nb, LANES))
              for m in range(nt_scan)]

        def step(t, carry):
            rows = pl.ds(pl.multiple_of(t * nb, nb), nb)
            out = []
            for m in range(nt_scan):
                hr, hi = carry[2 * m], carry[2 * m + 1]
                tre = j * nt_scan + m
                nr = ar[m] * hr - ai[m] * hi + hs_ref[tre, rows, :]
                ni = ar[m] * hi + ai[m] * hr + hs_ref[im0 + tre, rows, :]
                hs_ref[tre, rows, :] = nr
                hs_ref[im0 + tre, rows, :] = ni
                out += [nr, ni]
            return tuple(out)

        init = []
        for m in range(nt_scan):
            lanes = slice(j * cw + m * LANES, j * cw + (m + 1) * LANES)
            init += [st_ref[0, :, lanes], st_ref[1, :, lanes]]
        fin = lax.fori_loop(0, tt, step, tuple(init), unroll=8)
        for m in range(nt_scan):
            lanes = slice(j * cw + m * LANES, j * cw + (m + 1) * LANES)
            st_ref[0, :, lanes] = fin[2 * m]
            st_ref[1, :, lanes] = fin[2 * m + 1]

    for b in range(nb):
        rows = pl.ds(b, tt, stride=nb)
        for c in range(S5_CHUNKS):
            t0 = c * tiles_per_chunk
            hre = jnp.concatenate([hs_ref[t0 + m, rows, :] for m in range(tiles_per_chunk)], axis=-1)
            him = jnp.concatenate([hs_ref[im0 + t0 + m, rows, :] for m in range(tiles_per_chunk)], axis=-1)
            lanes = slice(c * LANES, (c + 1) * LANES)
            y = (_dot(hre.astype(BF16), cr_ref[c]) + _dot(him.astype(BF16), nci_ref[c])
                 + d_ref[:, lanes] * u_ref[b, :, lanes])
            z_ref[b, :, lanes] = _gelu_tanh(y).astype(z_ref.dtype)

    @pl.when(i == pl.num_programs(0) - 1)
    def _():
        hre_ref[...] = st_ref[0]
        him_ref[...] = st_ref[1]


def _s5_prompt(u, weights, d_skip):
    nb, nt, _ = u.shape
    tt = min(S5_TIME_BLOCK, nt)
    ar, ai, bbr, bbi, cr, nci = weights
    full2 = lambda i: (0, 0)
    full3 = lambda i: (0, 0, 0)
    cs = 8 * SSM_STATE
    body = functools.partial(_s5_prompt_body, nb=nb, tt=tt, cw=512)
    return pl.pallas_call(
        body,
        out_shape=[jax.ShapeDtypeStruct((nb, nt, SSM_WIDTH), BF16),
                   jax.ShapeDtypeStruct((nb, SSM_FLAT), F32),
                   jax.ShapeDtypeStruct((nb, SSM_FLAT), F32)],
        grid=(nt // tt,),
        in_specs=[pl.BlockSpec((nb, tt, SSM_WIDTH), lambda i: (0, i, 0)),
                  pl.BlockSpec((1, SSM_FLAT), full2), pl.BlockSpec((1, SSM_FLAT), full2),
                  pl.BlockSpec((S5_CHUNKS, LANES, cs), full3), pl.BlockSpec((S5_CHUNKS, LANES, cs), full3),
                  pl.BlockSpec((S5_CHUNKS, cs, LANES), full3), pl.BlockSpec((S5_CHUNKS, cs, LANES), full3),
                  pl.BlockSpec((1, SSM_WIDTH), full2)],
        out_specs=[pl.BlockSpec((nb, tt, SSM_WIDTH), lambda i: (0, i, 0)),
                   pl.BlockSpec((nb, SSM_FLAT), full2), pl.BlockSpec((nb, SSM_FLAT), full2)],
        scratch_shapes=[pltpu.VMEM((2 * SSM_FLAT // LANES, tt * nb, LANES), F32),
                        pltpu.VMEM((2, nb, SSM_FLAT), F32)],
        compiler_params=_params(("arbitrary",)),
        name="s5_prompt",
    )(u, ar, ai, bbr, bbi, cr, nci, d_skip.reshape(1, SSM_WIDTH))


def _s5_step_body(u_ref, h0r_ref, h0i_ref, ar_ref, ai_ref, bbr_ref, bbi_ref, cr_ref, nci_ref, d_ref,
                  z_ref, hre_ref, him_ref):
    cs = 8 * SSM_STATE
    u = u_ref[...]
    for c in range(S5_CHUNKS):
        cols = slice(c * cs, (c + 1) * cs)
        lanes = slice(c * LANES, (c + 1) * LANES)
        uc = u[:, lanes]
        ar = ar_ref[:, cols]
        ai = ai_ref[:, cols]
        h0r = h0r_ref[:, cols]
        h0i = h0i_ref[:, cols]
        hr = ar * h0r - ai * h0i + _dot_hp(uc, bbr_ref[c])
        hi = ar * h0i + ai * h0r + _dot_hp(uc, bbi_ref[c])
        hre_ref[:, cols] = hr
        him_ref[:, cols] = hi
        y = _dot_hp(hr, cr_ref[c]) + _dot_hp(hi, nci_ref[c]) + d_ref[:, lanes] * uc
        z_ref[:, lanes] = _gelu_tanh(y)


def _s5_step(u, h0_re, h0_im, weights, d_skip):
    n = u.shape[0]
    ar, ai, bbr, bbi, cr, nci = weights
    return pl.pallas_call(
        _s5_step_body,
        out_shape=[jax.ShapeDtypeStruct((n, SSM_WIDTH), F32),
                   jax.ShapeDtypeStruct((n, SSM_FLAT), F32),
                   jax.ShapeDtypeStruct((n, SSM_FLAT), F32)],
        name="s5_step",
    )(u, h0_re, h0_im, ar, ai, bbr, bbi, cr, nci, d_skip.reshape(1, SSM_WIDTH))


def _mix_body(attn_ref, z_ref, g_ref, x_ref, wup_ref, w1_ref, b1_ref, w2_ref, b2_ref, wout_ref,
              nf_ref, rw_ref, rb_ref, *rest, n_blocks, precise):
    h_ref, xn_ref, lg_ref = rest[-3:]

    @pl.when(pl.program_id(0) >= n_blocks)
    def _():
        h_ref[...] = jnp.zeros_like(h_ref)
        xn_ref[...] = jnp.zeros_like(xn_ref)
        lg_ref[...] = jnp.zeros_like(lg_ref)

    pl.when(pl.program_id(0) < n_blocks)(functools.partial(
        _mix_rows, attn_ref, z_ref, g_ref, x_ref, wup_ref, w1_ref, b1_ref, w2_ref, b2_ref, wout_ref,
        nf_ref, rw_ref, rb_ref, h_ref, xn_ref, lg_ref, precise=precise))


def _mix_rows(attn_ref, z_ref, g_ref, x_ref, wup_ref, w1_ref, b1_ref, w2_ref, b2_ref, wout_ref,
              nf_ref, rw_ref, rb_ref, h_ref, xn_ref, lg_ref, *, precise):
    dot = _dot_hp if precise else _dot
    operand = (lambda a: a.astype(F32)) if precise else (lambda a: a.astype(BF16))
    y_attn = dot(operand(attn_ref[...]), wup_ref[...])
    z = operand(z_ref[...])
    y_ssm = (dot(z, w1_ref[...]) + b1_ref[...]) * jax.nn.sigmoid(dot(z, w2_ref[...]) + b2_ref[...])
    g = g_ref[...]
    merged = jax.nn.sigmoid(g[:, :D_MODEL]) * y_attn + jax.nn.sigmoid(g[:, D_MODEL:]) * y_ssm
    h = x_ref[...] + dot(operand(merged), wout_ref[...])
    h_ref[...] = h
    ms = jnp.mean(h * h, axis=-1, keepdims=True)
    xn = h * lax.rsqrt(ms + RMS_EPS) * nf_ref[...]
    xn_ref[...] = xn
    lg_ref[...] = dot(operand(xn), rw_ref[...]) + rb_ref[...]


def _mix(attn, z, g, x, w, n_total, first_block, prev=None):
    n = x.shape[0]
    tm = ROW_BLOCK
    n_blocks = n // tm
    n_steps = n_blocks if prev is not None else n_total // tm - first_block
    row = lambda i: (jnp.minimum(i, n_blocks - 1), 0)
    orow = lambda i: (i + first_block, 0)
    full = lambda i: (0, 0)
    wspecs = [pl.BlockSpec(a.shape, full) for a in w]
    in_specs = [pl.BlockSpec((tm, ATTN_WIDTH), row), pl.BlockSpec((tm, SSM_WIDTH), row),
                pl.BlockSpec((tm, 2 * D_MODEL), row), pl.BlockSpec((tm, D_MODEL), row)] + wspecs
    args = [attn, z, g, x, *w]
    aliases = {}
    if prev is not None:
        base = len(args)
        in_specs = in_specs + [pl.BlockSpec(memory_space=pl.ANY)] * 3
        args = args + list(prev)
        aliases = {base: 0, base + 1: 1, base + 2: 2}
    return pl.pallas_call(
        functools.partial(_mix_body, n_blocks=n_blocks, precise=w[0].dtype == F32),
        out_shape=[jax.ShapeDtypeStruct((n_total, D_MODEL), F32),
                   jax.ShapeDtypeStruct((n_total, D_MODEL), F32),
                   jax.ShapeDtypeStruct((n_total, LANES), F32)],
        grid=(n_steps,),
        in_specs=in_specs,
        out_specs=[pl.BlockSpec((tm, D_MODEL), orow), pl.BlockSpec((tm, D_MODEL), orow),
                   pl.BlockSpec((tm, LANES), orow)],
        input_output_aliases=aliases,
        compiler_params=_params(("parallel",)),
        name="mix",
    )(*args)


def _route_body(lg_ref, sf_ref, blk_ref, cnt_ref, run_ref):
    tb = ROW_BLOCK
    i = pl.program_id(0)

    @pl.when(i == 0)
    def _():
        run_ref[...] = jnp.zeros_like(run_ref)

    lg = lg_ref[...]
    lane = lax.broadcasted_iota(jnp.int32, (tb, LANES), 1).astype(F32)
    work = lg
    tops, hots, idxs = [], [], []
    for _ in range(TOP_K):
        m = jnp.max(work, axis=-1, keepdims=True)
        idx = jnp.min(jnp.where(work == m, lane, float(LANES)), axis=-1, keepdims=True)
        hot = lane == idx
        work = jnp.where(hot, -jnp.inf, work)
        tops.append(m)
        hots.append(hot)
        idxs.append(idx)
    es = [jnp.exp(m - tops[0]) for m in tops]
    den = es[0] + es[1] + es[2] + es[3]
    sel = jnp.zeros((tb, LANES), F32)
    for hot in hots:
        sel = jnp.where(hot, 1.0, sel)
    r = lax.broadcasted_iota(jnp.int32, (tb, tb), 0)
    c = lax.broadcasted_iota(jnp.int32, (tb, tb), 1)
    earlier = (c < r).astype(BF16)
    sel_bf = sel.astype(BF16)
    lrank = _dot(earlier, sel_bf)
    lcnt = jnp.sum(sel, axis=0, keepdims=True)
    below = (lax.broadcasted_iota(jnp.int32, (LANES, LANES), 0)
             < lax.broadcasted_iota(jnp.int32, (LANES, LANES), 1)).astype(BF16)
    loff = _dot(jnp.broadcast_to(lcnt, (16, LANES)).astype(BF16), below)[0:1]
    lpos = lrank + loff
    sf = jnp.zeros((tb, LANES), F32)
    for k in range(TOP_K):
        pk = jnp.sum(jnp.where(hots[k], lpos, 0.0), axis=-1, keepdims=True)
        sf = jnp.where(lane == float(k), es[k] / den, sf)
        sf = jnp.where(lane == float(TOP_K + k), pk, sf)
    sf_ref[...] = sf
    blk_ref[0] = jnp.concatenate([lcnt, run_ref[...]] + [jnp.zeros((1, LANES), F32)] * 6, axis=0)
    run = run_ref[...] + lcnt
    run_ref[...] = run
    cnt_ref[...] = run


def _route(logits):
    n = logits.shape[0]
    tb = ROW_BLOCK
    row = lambda i: (i, 0)
    return pl.pallas_call(
        _route_body,
        out_shape=[jax.ShapeDtypeStruct((n, LANES), F32),
                   jax.ShapeDtypeStruct((n // tb, 8, LANES), F32),
                   jax.ShapeDtypeStruct((1, LANES), F32)],
        grid=(n // tb,),
        in_specs=[pl.BlockSpec((tb, LANES), row)],
        out_specs=[pl.BlockSpec((tb, LANES), row), pl.BlockSpec((1, 8, LANES), lambda i: (i, 0, 0)),
                   pl.BlockSpec((1, LANES), lambda i: (0, 0))],
        scratch_shapes=[pltpu.VMEM((1, LANES), F32)],
        compiler_params=_params(("arbitrary",)),
        name="route",
    )(logits)


ROW_TILE = D_MODEL // LANES
PAIRS = ROW_BLOCK * TOP_K
SEG_BITS = tuple(1 << b for b in range(ROW_BLOCK.bit_length() - 1, -1, -1))
PAD_BITS = tuple(1 << b for b in range(MOE_TILE.bit_length() - 2, -1, -1))


def _segment_copies(length, bits, make_copy):
    for bit in bits:
        @pl.when((length & bit) != 0)
        def _(bit=bit):
            make_copy(length & ~(2 * bit - 1), bit)


def _tile_rows(ref, first_row, n_rows):
    return ref.at[pl.ds(pl.multiple_of(first_row * ROW_TILE, ROW_TILE), n_rows * ROW_TILE)]


def _to_row_tiles(ref, value):
    n = value.shape[0]
    for s in range(ROW_TILE):
        ref[pl.ds(s, n, stride=ROW_TILE), :] = value[:, s * LANES:(s + 1) * LANES]


def _from_row_tiles(ref, n):
    return jnp.concatenate([ref[pl.ds(s, n, stride=ROW_TILE), :] for s in range(ROW_TILE)], axis=1)


def _dispatch_body(info_ref, pad_ref, sf_ref, x_ref, xs_ref, xl0_ref, xl1_ref, sem0, sem1):
    tb = ROW_BLOCK
    i = pl.program_id(0)

    @pl.when(i == 0)
    def _():
        xl1_ref[...] = jnp.zeros_like(xl1_ref)

        def fill(e, _):
            start = pad_ref[e]

            def copy(off, size):
                cp = pltpu.make_async_copy(_tile_rows(xl1_ref, 0, size), _tile_rows(xs_ref, start + off, size), sem1)
                cp.start()
                cp.wait()

            _segment_copies(pad_ref[N_EXPERTS + e], PAD_BITS, copy)
            return 0

        lax.fori_loop(0, N_EXPERTS, fill, 0)

        def clear_tile(t, _):
            cp = pltpu.make_async_copy(_tile_rows(xl1_ref, 0, MOE_TILE), _tile_rows(xs_ref, t * MOE_TILE, MOE_TILE), sem1)
            cp.start()
            cp.wait()
            return 0

        lax.fori_loop(pad_ref[2 * N_EXPERTS], xs_ref.shape[0] // (MOE_TILE * ROW_TILE), clear_tile, 0)

    def all_landed(xl_ref, sem):
        pltpu.make_async_copy(xl_ref, _tile_rows(xs_ref, 0, PAIRS), sem).wait()

    def step(xl_ref, sem, other_ref, other_sem):
        lpos_t = jnp.transpose(sf_ref[...])
        slot = lax.broadcasted_iota(jnp.int32, (PAIRS, tb), 0).astype(F32)
        onehot = jnp.zeros((PAIRS, tb), F32)
        for k in range(TOP_K):
            onehot = onehot + jnp.where(slot == lpos_t[TOP_K + k:TOP_K + k + 1, :], 1.0, 0.0)
        _to_row_tiles(xl_ref, _dot(onehot.astype(BF16), x_ref[...].astype(BF16)))

        def segment(e, _):
            length, src, dst = info_ref[0, 0, e], info_ref[0, 0, N_EXPERTS + e], info_ref[0, 0, 2 * N_EXPERTS + e]
            _segment_copies(length, SEG_BITS, lambda off, size: pltpu.make_async_copy(
                _tile_rows(xl_ref, src + off, size), _tile_rows(xs_ref, dst + off, size), sem).start())
            return 0

        lax.fori_loop(0, N_EXPERTS, segment, 0)

        @pl.when(i > 0)
        def _():
            all_landed(other_ref, other_sem)

        @pl.when(i == pl.num_programs(0) - 1)
        def _():
            all_landed(xl_ref, sem)

    pl.when(i % 2 == 0)(functools.partial(step, xl0_ref, sem0, xl1_ref, sem1))
    pl.when(i % 2 == 1)(functools.partial(step, xl1_ref, sem1, xl0_ref, sem0))


def _dispatch(info, pad, sf, xn, n_sorted):
    n = xn.shape[0]
    tb = ROW_BLOCK
    row = lambda i: (i, 0)
    return pl.pallas_call(
        _dispatch_body,
        out_shape=jax.ShapeDtypeStruct((n_sorted * ROW_TILE, LANES), F32),
        grid=(n // tb,),
        in_specs=[pl.BlockSpec((1, 1, LANES), lambda i: (i, 0, 0), memory_space=pltpu.SMEM),
                  pl.BlockSpec(memory_space=pltpu.SMEM),
                  pl.BlockSpec((tb, LANES), row),
                  pl.BlockSpec((tb, D_MODEL), row)],
        out_specs=pl.BlockSpec(memory_space=pl.ANY),
        scratch_shapes=[pltpu.VMEM((PAIRS * ROW_TILE, LANES), F32), pltpu.VMEM((PAIRS * ROW_TILE, LANES), F32),
                        pltpu.SemaphoreType.DMA, pltpu.SemaphoreType.DMA],
        compiler_params=_params(("arbitrary",)),
        name="dispatch",
    )(info, pad, sf, xn)


def _combine_body(info_ref, next_ref, sf_ref, h_ref, nw_ref, ys_ref, yp_ref, ys_out_ref,
                  yl0_ref, yl1_ref, sem0, sem1, *, n_prompt_blocks):
    tb = ROW_BLOCK
    i = pl.program_id(0)

    def fetch(blk_ref, yl_ref, sem):
        def segment(e, _):
            length, dst, src = blk_ref[0, 0, e], blk_ref[0, 0, N_EXPERTS + e], blk_ref[0, 0, 2 * N_EXPERTS + e]
            _segment_copies(length, SEG_BITS, lambda off, size: pltpu.make_async_copy(
                _tile_rows(ys_ref, src + off, size), _tile_rows(yl_ref, dst + off, size), sem).start())
            return 0

        lax.fori_loop(0, N_EXPERTS, segment, 0)

    def step(yl_ref, sem, other_ref, other_sem):
        @pl.when(i == 0)
        def _():
            fetch(info_ref, yl_ref, sem)

        @pl.when(i + 1 < pl.num_programs(0))
        def _():
            fetch(next_ref, other_ref, other_sem)

        sf = sf_ref[...]
        slot = lax.broadcasted_iota(jnp.int32, (tb, PAIRS), 1).astype(F32)
        gates = jnp.zeros((tb, PAIRS), F32)
        for k in range(TOP_K):
            gates = gates + jnp.where(slot == sf[:, TOP_K + k:TOP_K + k + 1], sf[:, k:k + 1], 0.0)
        g_hi, g_lo = _split_bf16(gates)
        pltpu.make_async_copy(_tile_rows(ys_ref, 0, PAIRS), yl_ref, sem).wait()
        y_hi, y_lo = _split_bf16(_from_row_tiles(yl_ref, PAIRS))
        out = h_ref[...] + (_dot(g_hi, y_hi) + _dot(g_lo, y_hi) + _dot(g_hi, y_lo))
        ms = jnp.mean(out * out, axis=-1, keepdims=True)
        y = out * lax.rsqrt(ms + RMS_EPS) * nw_ref[...]

        @pl.when(i < n_prompt_blocks)
        def _():
            yp_ref[...] = y

        @pl.when(i >= n_prompt_blocks)
        def _():
            ys_out_ref[...] = y

    pl.when(i % 2 == 0)(functools.partial(step, yl0_ref, sem0, yl1_ref, sem1))
    pl.when(i % 2 == 1)(functools.partial(step, yl1_ref, sem1, yl0_ref, sem0))


def _combine(info, sf, h, norm_w, ys, n_prompt):
    n = h.shape[0]
    tb = ROW_BLOCK
    nblk = n // tb
    npb = n_prompt // tb
    row = lambda i: (i, 0)
    return pl.pallas_call(
        functools.partial(_combine_body, n_prompt_blocks=npb),
        out_shape=[jax.ShapeDtypeStruct((n_prompt, D_MODEL), F32),
                   jax.ShapeDtypeStruct((n - n_prompt, D_MODEL), F32)],
        grid=(nblk,),
        in_specs=[pl.BlockSpec((1, 1, LANES), lambda i: (i, 0, 0), memory_space=pltpu.SMEM),
                  pl.BlockSpec((1, 1, LANES), lambda i: (jnp.minimum(i + 1, nblk - 1), 0, 0),
                               memory_space=pltpu.SMEM),
                  pl.BlockSpec((tb, LANES), row),
                  pl.BlockSpec((tb, D_MODEL), row),
                  pl.BlockSpec((1, D_MODEL), lambda i: (0, 0)),
                  pl.BlockSpec(memory_space=pl.ANY)],
        out_specs=[pl.BlockSpec((tb, D_MODEL), lambda i: (jnp.minimum(i, npb - 1), 0)),
                   pl.BlockSpec((tb, D_MODEL), lambda i: (jnp.maximum(i - npb, 0), 0))],
        scratch_shapes=[pltpu.VMEM((PAIRS * ROW_TILE, LANES), F32), pltpu.VMEM((PAIRS * ROW_TILE, LANES), F32),
                        pltpu.SemaphoreType.DMA, pltpu.SemaphoreType.DMA],
        compiler_params=_params(("arbitrary",)),
        name="combine",
    )(info, info, sf, h, norm_w, ys)


def _experts_body(te_ref, tv_ref, xs_ref, wu_ref, bu_ref, wd_ref, bd_ref, ys_ref, stage_ref, wdb_ref):
    i = pl.program_id(0)

    @pl.when((i == 0) | (te_ref[i] != te_ref[jnp.maximum(i - 1, 0)]))
    def _():
        half = D_FF // 2
        for c in range(ROW_TILE):
            lanes = slice(c * LANES, (c + 1) * LANES)
            stage_ref[c, pl.ds(0, half, stride=2), :] = wd_ref[0, :half, lanes]
            stage_ref[c, pl.ds(1, half, stride=2), :] = wd_ref[0, half:, lanes]
        wdb_ref[...] = jnp.concatenate([stage_ref[c] for c in range(ROW_TILE)], axis=1).astype(BF16)

    @pl.when(tv_ref[i] == 0)
    def _():
        ys_ref[...] = jnp.zeros_like(ys_ref)

    @pl.when(tv_ref[i] != 0)
    def _():
        x = _from_row_tiles(xs_ref, MOE_TILE).astype(BF16)
        even = lax.broadcasted_iota(jnp.int32, (MOE_TILE, LANES), 1) % 2 == 0
        acts = []
        for f in range(D_FF // MOE_FF_CHUNK):
            c1 = slice(f * MOE_FF_CHUNK, (f + 1) * MOE_FF_CHUNK)
            c2 = slice(D_FF + f * MOE_FF_CHUNK, D_FF + (f + 1) * MOE_FF_CHUNK)
            h1 = _dot(x, wu_ref[0, :, c1]) + bu_ref[0, :, c1]
            h2 = _dot(x, wu_ref[0, :, c2]) + bu_ref[0, :, c2]
            glu, lin = [], []
            for m in range(MOE_FF_CHUNK // LANES):
                a = h1[:, m * LANES:(m + 1) * LANES]
                b = h2[:, m * LANES:(m + 1) * LANES]
                glu.append(jnp.where(even, a, pltpu.roll(b, 1, 1)))
                lin.append(jnp.where(even, pltpu.roll(a, LANES - 1, 1), b))
            x_glu = jnp.minimum(jnp.concatenate(glu, axis=-1), SWIGLU_LIMIT)
            x_lin = jnp.clip(jnp.concatenate(lin, axis=-1), -SWIGLU_LIMIT, SWIGLU_LIMIT)
            acts.append((x_glu * jax.nn.sigmoid(SWIGLU_ALPHA * x_glu) * (x_lin + 1.0)).astype(BF16))
        _to_row_tiles(ys_ref, _dot(jnp.concatenate(acts, axis=-1), wdb_ref[...]) + bd_ref[0])


def _experts(tile_expert, tile_valid, xs, wu, bu, wd, bd):
    n_tiles = tile_expert.shape[0]
    tm = MOE_TILE
    wmap = lambda i, te, tv: (te[i], 0, 0)
    grid_spec = pltpu.PrefetchScalarGridSpec(
        num_scalar_prefetch=2,
        grid=(n_tiles,),
        in_specs=[pl.BlockSpec((tm * ROW_TILE, LANES), lambda i, te, tv: (i * tv[i], 0)),
                  pl.BlockSpec((1, D_MODEL, 2 * D_FF), wmap), pl.BlockSpec((1, 1, 2 * D_FF), wmap),
                  pl.BlockSpec((1, D_FF, D_MODEL), wmap), pl.BlockSpec((1, 1, D_MODEL), wmap)],
        out_specs=pl.BlockSpec((tm * ROW_TILE, LANES), lambda i, te, tv: (i, 0)),
        scratch_shapes=[pltpu.VMEM((ROW_TILE, D_FF, LANES), F32), pltpu.VMEM((D_FF, D_MODEL), BF16)],
    )
    return pl.pallas_call(
        _experts_body,
        out_shape=jax.ShapeDtypeStruct((n_tiles * tm * ROW_TILE, LANES), F32),
        grid_spec=grid_spec,
        compiler_params=_params(("arbitrary",)),
        name="experts",
    )(tile_expert, tile_valid, xs, wu, bu, wd, bd)


def _moe_plan(blk, counts, n_tiles):
    as_int = lambda a: a.astype(jnp.int32)
    seg_len = as_int(blk[:, 0, :N_EXPERTS])
    before = as_int(blk[:, 1, :N_EXPERTS])
    cnt = as_int(counts[0, :N_EXPERTS])
    tiles = (cnt + MOE_TILE - 1) // MOE_TILE
    ends = jnp.cumsum(tiles)
    offs = (ends - tiles) * MOE_TILE
    local = jnp.cumsum(seg_len, axis=1) - seg_len
    info = jnp.concatenate([seg_len, local, offs[None, :] + before,
                            jnp.zeros((seg_len.shape[0], LANES - 3 * N_EXPERTS), jnp.int32)], axis=1)
    pad = jnp.concatenate([offs + cnt, tiles * MOE_TILE - cnt, ends[-1:]])
    tile_id = jnp.arange(n_tiles, dtype=jnp.int32)
    tile_expert = jnp.minimum(jnp.sum((tile_id[:, None] >= ends[None, :]).astype(jnp.int32), axis=1),
                              N_EXPERTS - 1)
    tile_valid = (tile_id < ends[-1]).astype(jnp.int32)
    return info[:, None, :], pad, tile_expert, tile_valid


def _forward(x_prompt, x_sample, cache_k, cache_v, state_re, state_im, page_table,
             norm_mix, w_in, sb_bias, w_attn_up, lam_re, lam_im, log_dt, b_re, b_im, c_re, c_im,
             d_skip, glu_w1, glu_b1, glu_w2, glu_b2, w_out, norm_ffn, router_w, router_b,
             moe_w_up, moe_b_up, moe_w_down, moe_b_down, norm_final):
    nb, nt, _ = x_prompt.shape
    ns = x_sample.shape[0]
    n_p = nb * nt
    tb = ROW_BLOCK
    n_tot = n_p + tb
    row2 = lambda a: a.reshape(1, -1)

    w_in_bf = w_in.astype(BF16)
    mix_w32 = [w_attn_up, glu_w1, row2(glu_b1), glu_w2, row2(glu_b2), w_out, row2(norm_ffn),
               jnp.pad(router_w, ((0, 0), (0, LANES - N_EXPERTS))),
               jnp.pad(row2(router_b), ((0, 0), (0, LANES - N_EXPERTS)), constant_values=NEG_BIG)]
    mix_w = [a.astype(BF16) if a.shape[0] > 1 else a for a in mix_w32]
    wu = moe_w_up.astype(BF16)
    bu = moe_b_up[:, None, :]
    wd = moe_w_down
    bd = moe_b_down[:, None, :]
    ar, ai, bbr_t, bbi_t = _s5_prep(lam_re, lam_im, log_dt, b_re, b_im)
    s5w = _s5_weights(ar, ai, bbr_t, bbi_t, c_re, c_im, BF16)
    s5w32 = _s5_weights(ar, ai, bbr_t, bbi_t, c_re, c_im, F32)

    xp = x_prompt.reshape(n_p, D_MODEL)
    q, kt, vt, ktb, vtb, u, g = _inproj(xp, row2(norm_mix), w_in_bf, nb)
    attn = _attn_prompt(q, ktb, vtb, sb_bias, nb, nt)
    z, hre_p, him_p = _s5_prompt(u.reshape(nb, nt, SSM_WIDTH), s5w, d_skip)
    outs = _mix(attn, z.reshape(n_p, SSM_WIDTH), g, xp, mix_w, n_tot, 0)

    xs_pad = jnp.zeros((tb, D_MODEL), F32).at[:ns].set(x_sample.reshape(ns, D_MODEL))
    q_s, kt_s, vt_s, _, _, u_s, g_s = _inproj(xs_pad, row2(norm_mix), w_in, 1)
    n_pool = cache_k.shape[0]
    pages = lambda c: c.transpose(0, 2, 3, 1).reshape(n_pool, ATTN_WIDTH, PAGE_SIZE)
    attn_s = _attn_sample(q_s[:ns], pages(cache_k), pages(cache_v), page_table, sb_bias)
    z_s, hre_s, him_s = _s5_step(u_s[:ns], state_re.reshape(ns, SSM_FLAT), state_im.reshape(ns, SSM_FLAT),
                                 s5w32, d_skip)
    pad_rows = lambda a: jnp.zeros((tb, a.shape[1]), a.dtype).at[:ns].set(a)
    h, xn, logits = _mix(pad_rows(attn_s), pad_rows(z_s), g_s, xs_pad, mix_w32, n_tot, n_p // tb, prev=outs)

    sf, blk, counts = _route(logits)
    n_tiles = (n_tot * TOP_K) // MOE_TILE + N_EXPERTS
    info, pad, tile_expert, tile_valid = _moe_plan(blk, counts, n_tiles)
    xs = _dispatch(info, pad, sf, xn, n_tiles * MOE_TILE)
    ys = _experts(tile_expert, tile_valid, xs, wu, bu, wd, bd)
    y_p, y_s = _combine(info, sf, h, row2(norm_final), ys, n_p)

    heads = (N_HEADS, HEAD_DIM)
    state = (N_GROUPS, SSM_STATE)
    time_major = lambda a, b_, t_: a.reshape(b_, N_HEADS, HEAD_DIM, t_).transpose(0, 3, 1, 2)[None]
    return (y_p.reshape(nb, nt, D_MODEL), y_s[:ns].reshape(ns, 1, D_MODEL),
            time_major(kt, nb, nt), time_major(vt, nb, nt),
            hre_p.reshape(1, nb, *state), him_p.reshape(1, nb, *state),
            time_major(kt_s[:, :, :ns], 1, ns).reshape(1, ns, 1, *heads),
            time_major(vt_s[:, :, :ns], 1, ns).reshape(1, ns, 1, *heads),
            hre_s.reshape(1, ns, *state), him_s.reshape(1, ns, *state))


def kernel(x_prompt, x_sample, cache_k, cache_v, state_ssm_re, state_ssm_im, page_table, norm_mix, w_in, sb_bias, w_attn_up, ssm_lambda_re, ssm_lambda_im, ssm_log_dt, ssm_b_re, ssm_b_im, ssm_c_re, ssm_c_im, ssm_d, glu_w1, glu_b1, glu_w2, glu_b2, w_out, norm_ffn, router_w, router_b, moe_w_up, moe_b_up, moe_w_down, moe_b_down, norm_final):
    return _forward(x_prompt, x_sample, cache_k[0], cache_v[0], state_ssm_re[0], state_ssm_im[0], page_table,
                    norm_mix[0], w_in[0], sb_bias[0], w_attn_up[0], ssm_lambda_re[0], ssm_lambda_im[0],
                    ssm_log_dt[0], ssm_b_re[0], ssm_b_im[0], ssm_c_re[0], ssm_c_im[0], ssm_d[0],
                    glu_w1[0], glu_b1[0], glu_w2[0], glu_b2[0], w_out[0], norm_ffn[0], router_w[0],
                    router_b[0], moe_w_up[0], moe_b_up[0], moe_w_down[0], moe_b_down[0], norm_final)
```

```python
import functools
import math

import jax
import jax.numpy as jnp
from jax import lax
from jax.experimental import pallas as pl
from jax.experimental.pallas import tpu as pltpu

F32 = jnp.float32
BF16 = jnp.bfloat16

D_MODEL = 1024
N_HEADS = 8
HEAD_DIM = 64
ATTN_WIDTH = N_HEADS * HEAD_DIM
SSM_WIDTH = 512
SSM_GROUP = 16
N_GROUPS = SSM_WIDTH // SSM_GROUP
SSM_STATE = 64
SSM_FLAT = N_GROUPS * SSM_STATE
N_EXPERTS = 32
TOP_K = 4
D_FF = 1024
SWIGLU_LIMIT = 7.0
SWIGLU_ALPHA = 1.702
RMS_EPS = 1e-5
PAGE_SIZE = 128
IN_WIDTH = 3 * ATTN_WIDTH + SSM_WIDTH + 2 * D_MODEL

LANES = 128
ROW_BLOCK = 256
ATTN_BLOCK = 256
HEADS_PER_STEP = 4
S5_CHUNKS = SSM_WIDTH // LANES
S5_TIME_BLOCK = 128
MOE_TILE = 512
MOE_FF_CHUNK = 256
PAGES_PER_STEP = 16
NEG_BIG = -1e30
VMEM_LIMIT = 56 * 1024 * 1024


def _dot(a, b):
    return jnp.dot(a, b, preferred_element_type=F32)


def _dot_nt(a, b):
    return lax.dot_general(a, b, (((1,), (1,)), ((), ())), preferred_element_type=F32)


def _dot_hp(a, b):
    return jnp.dot(a, b, preferred_element_type=F32, precision=lax.Precision.HIGHEST)


def _dot_nt_hp(a, b):
    return lax.dot_general(a, b, (((1,), (1,)), ((), ())), preferred_element_type=F32,
                           precision=lax.Precision.HIGHEST)


def _softplus(z):
    return jnp.maximum(z, 0.0) + jnp.log(1.0 + jnp.exp2(jnp.abs(z) * (-math.log2(math.e))))


def _split_bf16(x):
    hi = x.astype(BF16)
    lo = (x - hi.astype(F32)).astype(BF16)
    return hi, lo


def _params(sem, vmem=VMEM_LIMIT):
    return pltpu.CompilerParams(dimension_semantics=sem, vmem_limit_bytes=vmem)


def _inproj_body(x_ref, nw_ref, wq_ref, wkv_ref, wug_ref, q_ref, kt_ref, vt_ref, ktb_ref, vtb_ref, u_ref, g_ref,
                 *, precise):
    dot, dot_nt = (_dot_hp, _dot_nt_hp) if precise else (_dot, _dot_nt)
    x = x_ref[...]
    ms = jnp.mean(x * x, axis=-1, keepdims=True)
    xn = x * lax.rsqrt(ms + RMS_EPS) * nw_ref[...]
    if not precise:
        xn = xn.astype(BF16)
    a = ATTN_WIDTH
    q_ref[...] = (dot(xn, wq_ref[...]) * (HEAD_DIM ** -0.5)).astype(BF16)
    kvt = dot_nt(wkv_ref[...], xn)
    kt_ref[0] = kvt[:a]
    ktb_ref[0] = kvt[:a].astype(BF16)
    vt_ref[0] = kvt[a:]
    vtb_ref[0] = kvt[a:].astype(BF16)
    ug = dot(xn, wug_ref[...])
    u_ref[...] = ug[:, :SSM_WIDTH]
    g_ref[...] = ug[:, SSM_WIDTH:]


def _inproj(x2d, norm_w, w_in_bf, nb):
    n = x2d.shape[0]
    nt = n // nb
    tm = ROW_BLOCK
    nq = nt // tm
    a = ATTN_WIDTH
    row = lambda b, i: (b * nq + i, 0)
    full = lambda b, i: (0, 0)
    tmaj = lambda b, i: (b, 0, i)
    wq = w_in_bf[:, :a]
    wkv_t = w_in_bf[:, a:3 * a].T
    wug = w_in_bf[:, 3 * a:]
    outs = [
        jax.ShapeDtypeStruct((n, a), BF16),
        jax.ShapeDtypeStruct((nb, a, nt), F32),
        jax.ShapeDtypeStruct((nb, a, nt), F32),
        jax.ShapeDtypeStruct((nb, a, nt), BF16),
        jax.ShapeDtypeStruct((nb, a, nt), BF16),
        jax.ShapeDtypeStruct((n, SSM_WIDTH), F32),
        jax.ShapeDtypeStruct((n, 2 * D_MODEL), F32),
    ]
    return pl.pallas_call(
        functools.partial(_inproj_body, precise=w_in_bf.dtype == F32),
        out_shape=outs,
        grid=(nb, nq),
        in_specs=[pl.BlockSpec((tm, D_MODEL), row),
                  pl.BlockSpec((1, D_MODEL), full),
                  pl.BlockSpec(wq.shape, full), pl.BlockSpec(wkv_t.shape, full), pl.BlockSpec(wug.shape, full)],
        out_specs=[pl.BlockSpec((tm, a), row)] + [pl.BlockSpec((1, a, tm), tmaj)] * 4
                  + [pl.BlockSpec((tm, SSM_WIDTH), row), pl.BlockSpec((tm, 2 * D_MODEL), row)],
        compiler_params=_params(("parallel", "parallel")),
        name="inproj",
    )(x2d, norm_w, wq, wkv_t, wug)


def _attn_body(bias_ref, q_ref, k_ref, v_ref, o_ref, acc_ref):
    tq = tk = ATTN_BLOCK
    hg = pl.program_id(1)
    qi = pl.program_id(2)
    nh = HEADS_PER_STEP
    r = lax.broadcasted_iota(jnp.int32, (tq, tk), 0)
    c = lax.broadcasted_iota(jnp.int32, (tq, tk), 1)
    later = (r > c).astype(BF16)
    causal = c < r
    rows = [slice(h * HEAD_DIM, (h + 1) * HEAD_DIM) for h in range(nh)]
    extra = 16
    col = lax.broadcasted_iota(jnp.int32, (tq, extra), 1)
    ones2 = jnp.where(col < 2, 1.0, 0.0).astype(BF16)
    qh = [jnp.concatenate([q_ref[:, rows[h]], ones2], axis=1) for h in range(nh)]
    rowi = lax.broadcasted_iota(jnp.int32, (extra, tk), 0)
    bias_rows = []
    for h in range(nh):
        b = jnp.full((extra, tk), bias_ref[hg * nh + h], F32)
        b_hi = b.astype(BF16).astype(F32)
        bias_rows.append(jnp.where(rowi == 0, b_hi, jnp.where(rowi == 1, b - b_hi, 0.0)).astype(BF16))
    acc_ref[...] = jnp.zeros_like(acc_ref)

    def block(kb, carries, diagonal):
        start = pl.multiple_of(kb * tk, tk)
        kblk = k_ref[0, :, pl.ds(start, tk)]
        vblk = v_ref[0, :, pl.ds(start, tk)]
        zs = [_dot(qh[h], jnp.concatenate([kblk[rows[h], :], bias_rows[h]], axis=0)) for h in range(nh)]
        sps = [_softplus(z) for z in zs]
        if diagonal:
            sps = [jnp.where(causal, sp, 0.0) for sp in sps]
        newers = []
        for sp in sps:
            hi, lo = _split_bf16(sp)
            newers.append(_dot(hi, later) + _dot(lo, later))
        new = []
        for h in range(nh):
            w = jnp.exp(zs[h] - sps[h] - newers[h])
            if diagonal:
                w = jnp.where(causal, w, 0.0)
            pv = _dot_nt(w.astype(BF16), vblk[rows[h], :])
            acc_ref[h] += pv if diagonal else jnp.exp(-carries[h]) * pv
            new.append(carries[h] + newers[h][:, 0:1] + sps[h][:, 0:1])
        return tuple(new)

    zero = tuple(jnp.zeros((tq, 1), F32) for _ in range(nh))
    carries = block(qi, zero, True)
    odd = qi % 2
    carries = lax.cond(odd == 1, lambda cs: block(qi - 1, cs, False), lambda cs: cs, carries)
    top = qi - 1 - odd

    def pair(j, cs):
        return block(top - 2 * j - 1, block(top - 2 * j, cs, False), False)

    lax.fori_loop(0, qi // 2, pair, carries)
    for h in range(nh):
        o_ref[:, rows[h]] = acc_ref[h].astype(o_ref.dtype)


def _attn_prompt(q, ktb, vtb, bias, nb, nt):
    tq = ATTN_BLOCK
    nq = nt // tq
    width = HEADS_PER_STEP * HEAD_DIM
    ng = ATTN_WIDTH // width
    return pl.pallas_call(
        _attn_body,
        out_shape=jax.ShapeDtypeStruct((nb * nt, ATTN_WIDTH), BF16),
        grid=(nb, ng, nq),
        in_specs=[pl.BlockSpec(memory_space=pltpu.SMEM),
                  pl.BlockSpec((tq, width), lambda b, g, i: (b * nq + i, g)),
                  pl.BlockSpec((1, width, nt), lambda b, g, i: (b, g, 0)),
                  pl.BlockSpec((1, width, nt), lambda b, g, i: (b, g, 0))],
        out_specs=pl.BlockSpec((tq, width), lambda b, g, i: (b * nq + i, g)),
        scratch_shapes=[pltpu.VMEM((HEADS_PER_STEP, tq, HEAD_DIM), F32)],
        compiler_params=_params(("parallel", "parallel", "arbitrary")),
        name="attn_prompt",
    )(bias, q, ktb, vtb)


def _attn_sample_body(pt_ref, bias_ref, q_ref, *refs):
    np_ = PAGES_PER_STEP
    k_refs = refs[:np_]
    v_refs = refs[np_:2 * np_]
    o_ref, carry_ref, acc_ref = refs[2 * np_:]
    c = pl.program_id(1)

    @pl.when(c == 0)
    def _():
        carry_ref[...] = jnp.zeros_like(carry_ref)
        acc_ref[...] = jnp.zeros_like(acc_ref)

    head_of_lane = lax.broadcasted_iota(jnp.int32, (N_HEADS, ATTN_WIDTH), 1) // HEAD_DIM
    head_of_row = lax.broadcasted_iota(jnp.int32, (N_HEADS, ATTN_WIDTH), 0)
    own = head_of_lane == head_of_row
    qrow = q_ref[0].astype(F32)
    qbd = jnp.where(own, jnp.broadcast_to(qrow, (N_HEADS, ATTN_WIDTH)), 0.0).astype(BF16)
    r = lax.broadcasted_iota(jnp.int32, (PAGE_SIZE, PAGE_SIZE), 0)
    cc = lax.broadcasted_iota(jnp.int32, (PAGE_SIZE, PAGE_SIZE), 1)
    later = (r > cc).astype(BF16)
    bias = bias_ref[...]
    kcat = jnp.concatenate([k_refs[i][0].astype(BF16) for i in range(np_)], axis=1)
    vcat = jnp.concatenate([v_refs[i][0].astype(BF16) for i in range(np_)], axis=1)
    z = _dot(qbd, kcat) + bias
    sp = _softplus(z)
    page = lambda a, i: a[:, i * PAGE_SIZE:(i + 1) * PAGE_SIZE]
    sp_rows = jnp.concatenate([page(sp, i) for i in range(np_)], axis=0)
    hi, lo = _split_bf16(sp_rows)
    newer_rows = _dot(hi, later) + _dot(lo, later)
    carry = carry_ref[...]
    ws = []
    for i in range(np_):
        sp_i = page(sp, i)
        newer_i = newer_rows[i * N_HEADS:(i + 1) * N_HEADS, :]
        ws.append(jnp.exp(page(z, i) - sp_i - newer_i - carry))
        carry = carry + newer_i[:, 0:1] + sp_i[:, 0:1]
    carry_ref[...] = carry
    w = jnp.concatenate(ws, axis=1).astype(BF16)
    acc = acc_ref[...] + _dot_nt(w, vcat)
    acc_ref[...] = acc

    @pl.when(c == pl.num_programs(1) - 1)
    def _():
        o_ref[0] = jnp.sum(jnp.where(own, acc, 0.0), axis=0, keepdims=True).astype(o_ref.dtype)


def _attn_sample(q, cache_k, cache_v, page_table, bias):
    nseq, n_pages = page_table.shape
    np_ = PAGES_PER_STEP
    nchunk = n_pages // np_

    def page_map(i):
        def f(b, c, pt):
            return (pt[b * n_pages + (n_pages - 1 - (c * np_ + i))], 0, 0)
        return f

    page_spec = [pl.BlockSpec((1, ATTN_WIDTH, PAGE_SIZE), page_map(i)) for i in range(np_)]
    grid_spec = pltpu.PrefetchScalarGridSpec(
        num_scalar_prefetch=1,
        grid=(nseq, nchunk),
        in_specs=[pl.BlockSpec((N_HEADS, 1), lambda b, c, pt: (0, 0)),
                  pl.BlockSpec((1, 1, ATTN_WIDTH), lambda b, c, pt: (b, 0, 0))]
                 + page_spec + page_spec,
        out_specs=pl.BlockSpec((1, 1, ATTN_WIDTH), lambda b, c, pt: (b, 0, 0)),
        scratch_shapes=[pltpu.VMEM((N_HEADS, 1), F32), pltpu.VMEM((N_HEADS, ATTN_WIDTH), F32)],
    )
    out = pl.pallas_call(
        _attn_sample_body,
        out_shape=jax.ShapeDtypeStruct((nseq, 1, ATTN_WIDTH), BF16),
        grid_spec=grid_spec,
        compiler_params=_params(("parallel", "arbitrary")),
        name="attn_sample",
    )(page_table.reshape(-1), bias.reshape(N_HEADS, 1), q.reshape(nseq, 1, ATTN_WIDTH),
      *([cache_k] * np_), *([cache_v] * np_))
    return out.reshape(nseq, ATTN_WIDTH)


def _s5_prep_body(lr_ref, li_ref, ldt_ref, br_ref, bi_ref, ar_ref, ai_ref, bbr_ref, bbi_ref):
    lr = lr_ref[...]
    li = li_ref[...]
    dt = jnp.exp(ldt_ref[...])
    mag = jnp.exp(lr * dt)
    ar = mag * jnp.cos(li * dt)
    ai = mag * jnp.sin(li * dt)
    den = lr * lr + li * li
    nr = ar - 1.0
    ni = ai
    fr = (nr * lr + ni * li) / den
    fi = (ni * lr - nr * li) / den
    ar_ref[...] = ar
    ai_ref[...] = ai
    br = br_ref[...]
    bi = bi_ref[...]
    bbr_ref[...] = fr[:, None, :] * br - fi[:, None, :] * bi
    bbi_ref[...] = fr[:, None, :] * bi + fi[:, None, :] * br


def _s5_prep(lam_re, lam_im, log_dt, b_re, b_im):
    g, p, h = N_GROUPS, SSM_STATE, SSM_GROUP
    outs = [jax.ShapeDtypeStruct((g, p), F32)] * 2 + [jax.ShapeDtypeStruct((g, h, p), F32)] * 2
    return pl.pallas_call(_s5_prep_body, out_shape=outs, name="s5_prep")(
        lam_re, lam_im, log_dt.reshape(g, 1),
        jnp.swapaxes(b_re, 1, 2), jnp.swapaxes(b_im, 1, 2))


def _s5_weights(ar, ai, bbr_t, bbi_t, c_re, c_im, dtype):
    eye = jnp.eye(8, dtype=F32)

    def drive(bb_t):
        x = bb_t.reshape(S5_CHUNKS, 8, SSM_GROUP, SSM_STATE)
        return (x[:, :, :, None, :] * eye[None, :, None, :, None]).reshape(
            S5_CHUNKS, LANES, 8 * SSM_STATE).astype(dtype)

    def readout(cm):
        x = jnp.swapaxes(cm, 1, 2).reshape(S5_CHUNKS, 8, SSM_STATE, SSM_GROUP)
        return (x[:, :, :, None, :] * eye[None, :, None, :, None]).reshape(
            S5_CHUNKS, 8 * SSM_STATE, LANES).astype(dtype)

    return (ar.reshape(1, SSM_FLAT), ai.reshape(1, SSM_FLAT), drive(bbr_t), drive(bbi_t),
            readout(c_re), readout(-c_im))


def _gelu_tanh(x):
    return 0.5 * x * (1.0 + jnp.tanh(math.sqrt(2.0 / math.pi) * (x + 0.044715 * (x * x * x))))


def _s5_prompt_body(u_ref, ar_ref, ai_ref, bbr_ref, bbi_ref, cr_ref, nci_ref, d_ref,
                    z_ref, hre_ref, him_ref, hs_ref, st_ref, *, nb, tt, cw):
    i = pl.program_id(0)
    cs = 8 * SSM_STATE
    tiles_per_chunk = cs // LANES
    im0 = SSM_FLAT // LANES

    @pl.when(i == 0)
    def _():
        st_ref[...] = jnp.zeros_like(st_ref)

    for b in range(nb):
        rows = pl.ds(b, tt, stride=nb)
        for c in range(S5_CHUNKS):
            uc = u_ref[b, :, c * LANES:(c + 1) * LANES].astype(BF16)
            dre = _dot(uc, bbr_ref[c])
            dim = _dot(uc, bbi_ref[c])
            for m in range(tiles_per_chunk):
                lanes = slice(m * LANES, (m + 1) * LANES)
                hs_ref[c * tiles_per_chunk + m, rows, :] = dre[:, lanes]
                hs_ref[im0 + c * tiles_per_chunk + m, rows, :] = dim[:, lanes]

    nt_scan = cw // LANES
    for j in range(SSM_FLAT // cw):
        cols = slice(j * cw, (j + 1) * cw)
        ar = [jnp.broadcast_to(ar_ref[:, j * cw + m * LANES:j * cw + (m + 1) * LANES], (nb, LANES))
              for m in range(nt_scan)]
        ai = [jnp.broadcast_to(ai_ref[:, j * cw + m * LANES:j * cw + (m + 1) * LANES], (nb, LANES))
              for m in range(nt_scan)]

        def step(t, carry):
            rows = pl.ds(pl.multiple_of(t * nb, nb), nb)
            out = []
            for m in range(nt_scan):
                hr, hi = carry[2 * m], carry[2 * m + 1]
                tre = j * nt_scan + m
                nr = ar[m] * hr - ai[m] * hi + hs_ref[tre, rows, :]
                ni = ar[m] * hi + ai[m] * hr + hs_ref[im0 + tre, rows, :]
                hs_ref[tre, rows, :] = nr
                hs_ref[im0 + tre, rows, :] = ni
                out += [nr, ni]
            return tuple(out)

        init = []
        for m in range(nt_scan):
            lanes = slice(j * cw + m * LANES, j * cw + (m + 1) * LANES)
            init += [st_ref[0, :, lanes], st_ref[1, :, lanes]]
        fin = lax.fori_loop(0, tt, step, tuple(init), unroll=8)
        for m in range(nt_scan):
            lanes = slice(j * cw + m * LANES, j * cw + (m + 1) * LANES)
            st_ref[0, :, lanes] = fin[2 * m]
            st_ref[1, :, lanes] = fin[2 * m + 1]

    for b in range(nb):
        rows = pl.ds(b, tt, stride=nb)
        for c in range(S5_CHUNKS):
            t0 = c * tiles_per_chunk
            hre = jnp.concatenate([hs_ref[t0 + m, rows, :] for m in range(tiles_per_chunk)], axis=-1)
            him = jnp.concatenate([hs_ref[im0 + t0 + m, rows, :] for m in range(tiles_per_chunk)], axis=-1)
            lanes = slice(c * LANES, (c + 1) * LANES)
            y = (_dot(hre.astype(BF16), cr_ref[c]) + _dot(him.astype(BF16), nci_ref[c])
                 + d_ref[:, lanes] * u_ref[b, :, lanes])
            z_ref[b, :, lanes] = _gelu_tanh(y).astype(z_ref.dtype)

    @pl.when(i == pl.num_programs(0) - 1)
    def _():
        hre_ref[...] = st_ref[0]
        him_ref[...] = st_ref[1]


def _s5_prompt(u, weights, d_skip):
    nb, nt, _ = u.shape
    tt = min(S5_TIME_BLOCK, nt)
    ar, ai, bbr, bbi, cr, nci = weights
    full2 = lambda i: (0, 0)
    full3 = lambda i: (0, 0, 0)
    cs = 8 * SSM_STATE
    body = functools.partial(_s5_prompt_body, nb=nb, tt=tt, cw=512)
    return pl.pallas_call(
        body,
        out_shape=[jax.ShapeDtypeStruct((nb, nt, SSM_WIDTH), BF16),
                   jax.ShapeDtypeStruct((nb, SSM_FLAT), F32),
                   jax.ShapeDtypeStruct((nb, SSM_FLAT), F32)],
        grid=(nt // tt,),
        in_specs=[pl.BlockSpec((nb, tt, SSM_WIDTH), lambda i: (0, i, 0)),
                  pl.BlockSpec((1, SSM_FLAT), full2), pl.BlockSpec((1, SSM_FLAT), full2),
                  pl.BlockSpec((S5_CHUNKS, LANES, cs), full3), pl.BlockSpec((S5_CHUNKS, LANES, cs), full3),
                  pl.BlockSpec((S5_CHUNKS, cs, LANES), full3), pl.BlockSpec((S5_CHUNKS, cs, LANES), full3),
                  pl.BlockSpec((1, SSM_WIDTH), full2)],
        out_specs=[pl.BlockSpec((nb, tt, SSM_WIDTH), lambda i: (0, i, 0)),
                   pl.BlockSpec((nb, SSM_FLAT), full2), pl.BlockSpec((nb, SSM_FLAT), full2)],
        scratch_shapes=[pltpu.VMEM((2 * SSM_FLAT // LANES, tt * nb, LANES), F32),
                        pltpu.VMEM((2, nb, SSM_FLAT), F32)],
        compiler_params=_params(("arbitrary",)),
        name="s5_prompt",
    )(u, ar, ai, bbr, bbi, cr, nci, d_skip.reshape(1, SSM_WIDTH))


def _s5_step_body(u_ref, h0r_ref, h0i_ref, ar_ref, ai_ref, bbr_ref, bbi_ref, cr_ref, nci_ref, d_ref,
                  z_ref, hre_ref, him_ref):
    cs = 8 * SSM_STATE
    u = u_ref[...]
    for c in range(S5_CHUNKS):
        cols = slice(c * cs, (c + 1) * cs)
        lanes = slice(c * LANES, (c + 1) * LANES)
        uc = u[:, lanes]
        ar = ar_ref[:, cols]
        ai = ai_ref[:, cols]
        h0r = h0r_ref[:, cols]
        h0i = h0i_ref[:, cols]
        hr = ar * h0r - ai * h0i + _dot_hp(uc, bbr_ref[c])
        hi = ar * h0i + ai * h0r + _dot_hp(uc, bbi_ref[c])
        hre_ref[:, cols] = hr
        him_ref[:, cols] = hi
        y = _dot_hp(hr, cr_ref[c]) + _dot_hp(hi, nci_ref[c]) + d_ref[:, lanes] * uc
        z_ref[:, lanes] = _gelu_tanh(y)


def _s5_step(u, h0_re, h0_im, weights, d_skip):
    n = u.shape[0]
    ar, ai, bbr, bbi, cr, nci = weights
    return pl.pallas_call(
        _s5_step_body,
        out_shape=[jax.ShapeDtypeStruct((n, SSM_WIDTH), F32),
                   jax.ShapeDtypeStruct((n, SSM_FLAT), F32),
                   jax.ShapeDtypeStruct((n, SSM_FLAT), F32)],
        name="s5_step",
    )(u, h0_re, h0_im, ar, ai, bbr, bbi, cr, nci, d_skip.reshape(1, SSM_WIDTH))


def _mix_body(attn_ref, z_ref, g_ref, x_ref, wup_ref, w1_ref, b1_ref, w2_ref, b2_ref, wout_ref,
              nf_ref, rw_ref, rb_ref, *rest, n_blocks, precise):
    h_ref, xn_ref, lg_ref = rest[-3:]

    @pl.when(pl.program_id(0) >= n_blocks)
    def _():
        h_ref[...] = jnp.zeros_like(h_ref)
        xn_ref[...] = jnp.zeros_like(xn_ref)
        lg_ref[...] = jnp.zeros_like(lg_ref)

    pl.when(pl.program_id(0) < n_blocks)(functools.partial(
        _mix_rows, attn_ref, z_ref, g_ref, x_ref, wup_ref, w1_ref, b1_ref, w2_ref, b2_ref, wout_ref,
        nf_ref, rw_ref, rb_ref, h_ref, xn_ref, lg_ref, precise=precise))


def _mix_rows(attn_ref, z_ref, g_ref, x_ref, wup_ref, w1_ref, b1_ref, w2_ref, b2_ref, wout_ref,
              nf_ref, rw_ref, rb_ref, h_ref, xn_ref, lg_ref, *, precise):
    dot = _dot_hp if precise else _dot
    operand = (lambda a: a.astype(F32)) if precise else (lambda a: a.astype(BF16))
    y_attn = dot(operand(attn_ref[...]), wup_ref[...])
    z = operand(z_ref[...])
    y_ssm = (dot(z, w1_ref[...]) + b1_ref[...]) * jax.nn.sigmoid(dot(z, w2_ref[...]) + b2_ref[...])
    g = g_ref[...]
    merged = jax.nn.sigmoid(g[:, :D_MODEL]) * y_attn + jax.nn.sigmoid(g[:, D_MODEL:]) * y_ssm
    h = x_ref[...] + dot(operand(merged), wout_ref[...])
    h_ref[...] = h
    ms = jnp.mean(h * h, axis=-1, keepdims=True)
    xn = h * lax.rsqrt(ms + RMS_EPS) * nf_ref[...]
    xn_ref[...] = xn
    lg_ref[...] = dot(operand(xn), rw_ref[...]) + rb_ref[...]


def _mix(attn, z, g, x, w, n_total, first_block, prev=None):
    n = x.shape[0]
    tm = ROW_BLOCK
    n_blocks = n // tm
    n_steps = n_blocks if prev is not None else n_total // tm - first_block
    row = lambda i: (jnp.minimum(i, n_blocks - 1), 0)
    orow = lambda i: (i + first_block, 0)
    full = lambda i: (0, 0)
    wspecs = [pl.BlockSpec(a.shape, full) for a in w]
    in_specs = [pl.BlockSpec((tm, ATTN_WIDTH), row), pl.BlockSpec((tm, SSM_WIDTH), row),
                pl.BlockSpec((tm, 2 * D_MODEL), row), pl.BlockSpec((tm, D_MODEL), row)] + wspecs
    args = [attn, z, g, x, *w]
    aliases = {}
    if prev is not None:
        base = len(args)
        in_specs = in_specs + [pl.BlockSpec(memory_space=pl.ANY)] * 3
        args = args + list(prev)
        aliases = {base: 0, base + 1: 1, base + 2: 2}
    return pl.pallas_call(
        functools.partial(_mix_body, n_blocks=n_blocks, precise=w[0].dtype == F32),
        out_shape=[jax.ShapeDtypeStruct((n_total, D_MODEL), F32),
                   jax.ShapeDtypeStruct((n_total, D_MODEL), F32),
                   jax.ShapeDtypeStruct((n_total, LANES), F32)],
        grid=(n_steps,),
        in_specs=in_specs,
        out_specs=[pl.BlockSpec((tm, D_MODEL), orow), pl.BlockSpec((tm, D_MODEL), orow),
                   pl.BlockSpec((tm, LANES), orow)],
        input_output_aliases=aliases,
        compiler_params=_params(("parallel",)),
        name="mix",
    )(*args)


def _route_body(lg_ref, sf_ref, blk_ref, cnt_ref, run_ref):
    tb = ROW_BLOCK
    i = pl.program_id(0)

    @pl.when(i == 0)
    def _():
        run_ref[...] = jnp.zeros_like(run_ref)

    lg = lg_ref[...]
    lane = lax.broadcasted_iota(jnp.int32, (tb, LANES), 1).astype(F32)
    work = lg
    tops, hots, idxs = [], [], []
    for _ in range(TOP_K):
        m = jnp.max(work, axis=-1, keepdims=True)
        idx = jnp.min(jnp.where(work == m, lane, float(LANES)), axis=-1, keepdims=True)
        hot = lane == idx
        work = jnp.where(hot, -jnp.inf, work)
        tops.append(m)
        hots.append(hot)
        idxs.append(idx)
    es = [jnp.exp(m - tops[0]) for m in tops]
    den = es[0] + es[1] + es[2] + es[3]
    sel = jnp.zeros((tb, LANES), F32)
    for hot in hots:
        sel = jnp.where(hot, 1.0, sel)
    r = lax.broadcasted_iota(jnp.int32, (tb, tb), 0)
    c = lax.broadcasted_iota(jnp.int32, (tb, tb), 1)
    earlier = (c < r).astype(BF16)
    sel_bf = sel.astype(BF16)
    lrank = _dot(earlier, sel_bf)
    lcnt = jnp.sum(sel, axis=0, keepdims=True)
    below = (lax.broadcasted_iota(jnp.int32, (LANES, LANES), 0)
             < lax.broadcasted_iota(jnp.int32, (LANES, LANES), 1)).astype(BF16)
    loff = _dot(jnp.broadcast_to(lcnt, (16, LANES)).astype(BF16), below)[0:1]
    lpos = lrank + loff
    sf = jnp.zeros((tb, LANES), F32)
    for k in range(TOP_K):
        pk = jnp.sum(jnp.where(hots[k], lpos, 0.0), axis=-1, keepdims=True)
        sf = jnp.where(lane == float(k), es[k] / den, sf)
        sf = jnp.where(lane == float(TOP_K + k), pk, sf)
    sf_ref[...] = sf
    blk_ref[0] = jnp.concatenate([lcnt, run_ref[...]] + [jnp.zeros((1, LANES), F32)] * 6, axis=0)
    run = run_ref[...] + lcnt
    run_ref[...] = run
    cnt_ref[...] = run


def _route(logits):
    n = logits.shape[0]
    tb = ROW_BLOCK
    row = lambda i: (i, 0)
    return pl.pallas_call(
        _route_body,
        out_shape=[jax.ShapeDtypeStruct((n, LANES), F32),
                   jax.ShapeDtypeStruct((n // tb, 8, LANES), F32),
                   jax.ShapeDtypeStruct((1, LANES), F32)],
        grid=(n // tb,),
        in_specs=[pl.BlockSpec((tb, LANES), row)],
        out_specs=[pl.BlockSpec((tb, LANES), row), pl.BlockSpec((1, 8, LANES), lambda i: (i, 0, 0)),
                   pl.BlockSpec((1, LANES), lambda i: (0, 0))],
        scratch_shapes=[pltpu.VMEM((1, LANES), F32)],
        compiler_params=_params(("arbitrary",)),
        name="route",
    )(logits)


ROW_TILE = D_MODEL // LANES
PAIRS = ROW_BLOCK * TOP_K
SEG_BITS = tuple(1 << b for b in range(ROW_BLOCK.bit_length() - 1, -1, -1))
PAD_BITS = tuple(1 << b for b in range(MOE_TILE.bit_length() - 2, -1, -1))


def _segment_copies(length, bits, make_copy):
    for bit in bits:
        @pl.when((length & bit) != 0)
        def _(bit=bit):
            make_copy(length & ~(2 * bit - 1), bit)


def _tile_rows(ref, first_row, n_rows):
    return ref.at[pl.ds(pl.multiple_of(first_row * ROW_TILE, ROW_TILE), n_rows * ROW_TILE)]


def _to_row_tiles(ref, value):
    n = value.shape[0]
    for s in range(ROW_TILE):
        ref[pl.ds(s, n, stride=ROW_TILE), :] = value[:, s * LANES:(s + 1) * LANES]


def _from_row_tiles(ref, n):
    return jnp.concatenate([ref[pl.ds(s, n, stride=ROW_TILE), :] for s in range(ROW_TILE)], axis=1)


def _dispatch_body(info_ref, pad_ref, sf_ref, x_ref, xs_ref, xl0_ref, xl1_ref, sem0, sem1):
    tb = ROW_BLOCK
    i = pl.program_id(0)

    @pl.when(i == 0)
    def _():
        xl1_ref[...] = jnp.zeros_like(xl1_ref)

        def fill(e, _):
            start = pad_ref[e]

            def copy(off, size):
                cp = pltpu.make_async_copy(_tile_rows(xl1_ref, 0, size), _tile_rows(xs_ref, start + off, size), sem1)
                cp.start()
                cp.wait()

            _segment_copies(pad_ref[N_EXPERTS + e], PAD_BITS, copy)
            return 0

        lax.fori_loop(0, N_EXPERTS, fill, 0)

        def clear_tile(t, _):
            cp = pltpu.make_async_copy(_tile_rows(xl1_ref, 0, MOE_TILE), _tile_rows(xs_ref, t * MOE_TILE, MOE_TILE), sem1)
            cp.start()
            cp.wait()
            return 0

        lax.fori_loop(pad_ref[2 * N_EXPERTS], xs_ref.shape[0] // (MOE_TILE * ROW_TILE), clear_tile, 0)

    def all_landed(xl_ref, sem):
        pltpu.make_async_copy(xl_ref, _tile_rows(xs_ref, 0, PAIRS), sem).wait()

    def step(xl_ref, sem, other_ref, other_sem):
        lpos_t = jnp.transpose(sf_ref[...])
        slot = lax.broadcasted_iota(jnp.int32, (PAIRS, tb), 0).astype(F32)
        onehot = jnp.zeros((PAIRS, tb), F32)
        for k in range(TOP_K):
            onehot = onehot + jnp.where(slot == lpos_t[TOP_K + k:TOP_K + k + 1, :], 1.0, 0.0)
        _to_row_tiles(xl_ref, _dot(onehot.astype(BF16), x_ref[...].astype(BF16)))

        def segment(e, _):
            length, src, dst = info_ref[0, 0, e], info_ref[0, 0, N_EXPERTS + e], info_ref[0, 0, 2 * N_EXPERTS + e]
            _segment_copies(length, SEG_BITS, lambda off, size: pltpu.make_async_copy(
                _tile_rows(xl_ref, src + off, size), _tile_rows(xs_ref, dst + off, size), sem).start())
            return 0

        lax.fori_loop(0, N_EXPERTS, segment, 0)

        @pl.when(i > 0)
        def _():
            all_landed(other_ref, other_sem)

        @pl.when(i == pl.num_programs(0) - 1)
        def _():
            all_landed(xl_ref, sem)

    pl.when(i % 2 == 0)(functools.partial(step, xl0_ref, sem0, xl1_ref, sem1))
    pl.when(i % 2 == 1)(functools.partial(step, xl1_ref, sem1, xl0_ref, sem0))


def _dispatch(info, pad, sf, xn, n_sorted):
    n = xn.shape[0]
    tb = ROW_BLOCK
    row = lambda i: (i, 0)
    return pl.pallas_call(
        _dispatch_body,
        out_shape=jax.ShapeDtypeStruct((n_sorted * ROW_TILE, LANES), F32),
        grid=(n // tb,),
        in_specs=[pl.BlockSpec((1, 1, LANES), lambda i: (i, 0, 0), memory_space=pltpu.SMEM),
                  pl.BlockSpec(memory_space=pltpu.SMEM),
                  pl.BlockSpec((tb, LANES), row),
                  pl.BlockSpec((tb, D_MODEL), row)],
        out_specs=pl.BlockSpec(memory_space=pl.ANY),
        scratch_shapes=[pltpu.VMEM((PAIRS * ROW_TILE, LANES), F32), pltpu.VMEM((PAIRS * ROW_TILE, LANES), F32),
                        pltpu.SemaphoreType.DMA, pltpu.SemaphoreType.DMA],
        compiler_params=_params(("arbitrary",)),
        name="dispatch",
    )(info, pad, sf, xn)


def _combine_body(info_ref, next_ref, sf_ref, h_ref, nw_ref, ys_ref, yp_ref, ys_out_ref,
                  yl0_ref, yl1_ref, sem0, sem1, *, n_prompt_blocks):
    tb = ROW_BLOCK
    i = pl.program_id(0)

    def fetch(blk_ref, yl_ref, sem):
        def segment(e, _):
            length, dst, src = blk_ref[0, 0, e], blk_ref[0, 0, N_EXPERTS + e], blk_ref[0, 0, 2 * N_EXPERTS + e]
            _segment_copies(length, SEG_BITS, lambda off, size: pltpu.make_async_copy(
                _tile_rows(ys_ref, src + off, size), _tile_rows(yl_ref, dst + off, size), sem).start())
            return 0

        lax.fori_loop(0, N_EXPERTS, segment, 0)

    def step(yl_ref, sem, other_ref, other_sem):
        @pl.when(i == 0)
        def _():
            fetch(info_ref, yl_ref, sem)

        @pl.when(i + 1 < pl.num_programs(0))
        def _():
            fetch(next_ref, other_ref, other_sem)

        sf = sf_ref[...]
        slot = lax.broadcasted_iota(jnp.int32, (tb, PAIRS), 1).astype(F32)
        gates = jnp.zeros((tb, PAIRS), F32)
        for k in range(TOP_K):
            gates = gates + jnp.where(slot == sf[:, TOP_K + k:TOP_K + k + 1], sf[:, k:k + 1], 0.0)
        g_hi, g_lo = _split_bf16(gates)
        pltpu.make_async_copy(_tile_rows(ys_ref, 0, PAIRS), yl_ref, sem).wait()
        y_hi, y_lo = _split_bf16(_from_row_tiles(yl_ref, PAIRS))
        out = h_ref[...] + (_dot(g_hi, y_hi) + _dot(g_lo, y_hi) + _dot(g_hi, y_lo))
        ms = jnp.mean(out * out, axis=-1, keepdims=True)
        y = out * lax.rsqrt(ms + RMS_EPS) * nw_ref[...]

        @pl.when(i < n_prompt_blocks)
        def _():
            yp_ref[...] = y

        @pl.when(i >= n_prompt_blocks)
        def _():
            ys_out_ref[...] = y

    pl.when(i % 2 == 0)(functools.partial(step, yl0_ref, sem0, yl1_ref, sem1))
    pl.when(i % 2 == 1)(functools.partial(step, yl1_ref, sem1, yl0_ref, sem0))


def _combine(info, sf, h, norm_w, ys, n_prompt):
    n = h.shape[0]
    tb = ROW_BLOCK
    nblk = n // tb
    npb = n_prompt // tb
    row = lambda i: (i, 0)
    return pl.pallas_call(
        functools.partial(_combine_body, n_prompt_blocks=npb),
        out_shape=[jax.ShapeDtypeStruct((n_prompt, D_MODEL), F32),
                   jax.ShapeDtypeStruct((n - n_prompt, D_MODEL), F32)],
        grid=(nblk,),
        in_specs=[pl.BlockSpec((1, 1, LANES), lambda i: (i, 0, 0), memory_space=pltpu.SMEM),
                  pl.BlockSpec((1, 1, LANES), lambda i: (jnp.minimum(i + 1, nblk - 1), 0, 0),
                               memory_space=pltpu.SMEM),
                  pl.BlockSpec((tb, LANES), row),
                  pl.BlockSpec((tb, D_MODEL), row),
                  pl.BlockSpec((1, D_MODEL), lambda i: (0, 0)),
                  pl.BlockSpec(memory_space=pl.ANY)],
        out_specs=[pl.BlockSpec((tb, D_MODEL), lambda i: (jnp.minimum(i, npb - 1), 0)),
                   pl.BlockSpec((tb, D_MODEL), lambda i: (jnp.maximum(i - npb, 0), 0))],
        scratch_shapes=[pltpu.VMEM((PAIRS * ROW_TILE, LANES), F32), pltpu.VMEM((PAIRS * ROW_TILE, LANES), F32),
                        pltpu.SemaphoreType.DMA, pltpu.SemaphoreType.DMA],
        compiler_params=_params(("arbitrary",)),
        name="combine",
    )(info, info, sf, h, norm_w, ys)


def _experts_body(te_ref, tv_ref, xs_ref, wu_ref, bu_ref, wd_ref, bd_ref, ys_ref, stage_ref, wdb_ref, wub_ref):
    i = pl.program_id(0)

    @pl.when((i == 0) | (te_ref[i] != te_ref[jnp.maximum(i - 1, 0)]))
    def _():
        wub_ref[...] = wu_ref[0].astype(BF16)
        half = D_FF // 2
        for c in range(ROW_TILE):
            lanes = slice(c * LANES, (c + 1) * LANES)
            stage_ref[c, pl.ds(0, half, stride=2), :] = wd_ref[0, :half, lanes]
            stage_ref[c, pl.ds(1, half, stride=2), :] = wd_ref[0, half:, lanes]
        wdb_ref[...] = jnp.concatenate([stage_ref[c] for c in range(ROW_TILE)], axis=1).astype(BF16)

    @pl.when(tv_ref[i] == 0)
    def _():
        ys_ref[...] = jnp.zeros_like(ys_ref)

    @pl.when(tv_ref[i] != 0)
    def _():
        x = _from_row_tiles(xs_ref, MOE_TILE).astype(BF16)
        even = lax.broadcasted_iota(jnp.int32, (MOE_TILE, LANES), 1) % 2 == 0
        acts = []
        for f in range(D_FF // MOE_FF_CHUNK):
            c1 = slice(f * MOE_FF_CHUNK, (f + 1) * MOE_FF_CHUNK)
            c2 = slice(D_FF + f * MOE_FF_CHUNK, D_FF + (f + 1) * MOE_FF_CHUNK)
            h1 = _dot(x, wub_ref[:, c1]) + bu_ref[0, :, c1]
            h2 = _dot(x, wub_ref[:, c2]) + bu_ref[0, :, c2]
            glu, lin = [], []
            for m in range(MOE_FF_CHUNK // LANES):
                a = h1[:, m * LANES:(m + 1) * LANES]
                b = h2[:, m * LANES:(m + 1) * LANES]
                glu.append(jnp.where(even, a, pltpu.roll(b, 1, 1)))
                lin.append(jnp.where(even, pltpu.roll(a, LANES - 1, 1), b))
            x_glu = jnp.minimum(jnp.concatenate(glu, axis=-1), SWIGLU_LIMIT)
            x_lin = jnp.clip(jnp.concatenate(lin, axis=-1), -SWIGLU_LIMIT, SWIGLU_LIMIT)
            acts.append((x_glu * jax.nn.sigmoid(SWIGLU_ALPHA * x_glu) * (x_lin + 1.0)).astype(BF16))
        _to_row_tiles(ys_ref, _dot(jnp.concatenate(acts, axis=-1), wdb_ref[...]) + bd_ref[0])


def _experts(tile_expert, tile_valid, xs, wu, bu, wd, bd):
    n_tiles = tile_expert.shape[0]
    tm = MOE_TILE
    wmap = lambda i, te, tv: (te[i], 0, 0)
    grid_spec = pltpu.PrefetchScalarGridSpec(
        num_scalar_prefetch=2,
        grid=(n_tiles,),
        in_specs=[pl.BlockSpec((tm * ROW_TILE, LANES), lambda i, te, tv: (i * tv[i], 0)),
                  pl.BlockSpec((1, D_MODEL, 2 * D_FF), wmap), pl.BlockSpec((1, 1, 2 * D_FF), wmap),
                  pl.BlockSpec((1, D_FF, D_MODEL), wmap), pl.BlockSpec((1, 1, D_MODEL), wmap)],
        out_specs=pl.BlockSpec((tm * ROW_TILE, LANES), lambda i, te, tv: (i, 0)),
        scratch_shapes=[pltpu.VMEM((ROW_TILE, D_FF, LANES), F32), pltpu.VMEM((D_FF, D_MODEL), BF16),
                        pltpu.VMEM((D_MODEL, 2 * D_FF), BF16)],
    )
    return pl.pallas_call(
        _experts_body,
        out_shape=jax.ShapeDtypeStruct((n_tiles * tm * ROW_TILE, LANES), F32),
        grid_spec=grid_spec,
        compiler_params=_params(("arbitrary",)),
        name="experts",
    )(tile_expert, tile_valid, xs, wu, bu, wd, bd)


def _moe_plan(blk, counts, n_tiles):
    as_int = lambda a: a.astype(jnp.int32)
    seg_len = as_int(blk[:, 0, :N_EXPERTS])
    before = as_int(blk[:, 1, :N_EXPERTS])
    cnt = as_int(counts[0, :N_EXPERTS])
    tiles = (cnt + MOE_TILE - 1) // MOE_TILE
    ends = jnp.cumsum(tiles)
    offs = (ends - tiles) * MOE_TILE
    local = jnp.cumsum(seg_len, axis=1) - seg_len
    info = jnp.concatenate([seg_len, local, offs[None, :] + before,
                            jnp.zeros((seg_len.shape[0], LANES - 3 * N_EXPERTS), jnp.int32)], axis=1)
    pad = jnp.concatenate([offs + cnt, tiles * MOE_TILE - cnt, ends[-1:]])
    tile_id = jnp.arange(n_tiles, dtype=jnp.int32)
    tile_expert = jnp.minimum(jnp.sum((tile_id[:, None] >= ends[None, :]).astype(jnp.int32), axis=1),
                              N_EXPERTS - 1)
    tile_valid = (tile_id < ends[-1]).astype(jnp.int32)
    return info[:, None, :], pad, tile_expert, tile_valid


def _forward(x_prompt, x_sample, cache_k, cache_v, state_re, state_im, page_table,
             norm_mix, w_in, sb_bias, w_attn_up, lam_re, lam_im, log_dt, b_re, b_im, c_re, c_im,
             d_skip, glu_w1, glu_b1, glu_w2, glu_b2, w_out, norm_ffn, router_w, router_b,
             moe_w_up, moe_b_up, moe_w_down, moe_b_down, norm_final):
    nb, nt, _ = x_prompt.shape
    ns = x_sample.shape[0]
    n_p = nb * nt
    tb = ROW_BLOCK
    n_tot = n_p + tb
    row2 = lambda a: a.reshape(1, -1)

    w_in_bf = w_in.astype(BF16)
    mix_w32 = [w_attn_up, glu_w1, row2(glu_b1), glu_w2, row2(glu_b2), w_out, row2(norm_ffn),
               jnp.pad(router_w, ((0, 0), (0, LANES - N_EXPERTS))),
               jnp.pad(row2(router_b), ((0, 0), (0, LANES - N_EXPERTS)), constant_values=NEG_BIG)]
    mix_w = [a.astype(BF16) if a.shape[0] > 1 else a for a in mix_w32]
    wu = moe_w_up
    bu = moe_b_up[:, None, :]
    wd = moe_w_down
    bd = moe_b_down[:, None, :]
    ar, ai, bbr_t, bbi_t = _s5_prep(lam_re, lam_im, log_dt, b_re, b_im)
    s5w = _s5_weights(ar, ai, bbr_t, bbi_t, c_re, c_im, BF16)
    s5w32 = _s5_weights(ar, ai, bbr_t, bbi_t, c_re, c_im, F32)

    xp = x_prompt.reshape(n_p, D_MODEL)
    q, kt, vt, ktb, vtb, u, g = _inproj(xp, row2(norm_mix), w_in_bf, nb)
    attn = _attn_prompt(q, ktb, vtb, sb_bias, nb, nt)
    z, hre_p, him_p = _s5_prompt(u.reshape(nb, nt, SSM_WIDTH), s5w, d_skip)
    outs = _mix(attn, z.reshape(n_p, SSM_WIDTH), g, xp, mix_w, n_tot, 0)

    xs_pad = jnp.zeros((tb, D_MODEL), F32).at[:ns].set(x_sample.reshape(ns, D_MODEL))
    q_s, kt_s, vt_s, _, _, u_s, g_s = _inproj(xs_pad, row2(norm_mix), w_in, 1)
    n_pool = cache_k.shape[0]
    pages = lambda c: c.transpose(0, 2, 3, 1).reshape(n_pool, ATTN_WIDTH, PAGE_SIZE)
    attn_s = _attn_sample(q_s[:ns], pages(cache_k), pages(cache_v), page_table, sb_bias)
    z_s, hre_s, him_s = _s5_step(u_s[:ns], state_re.reshape(ns, SSM_FLAT), state_im.reshape(ns, SSM_FLAT),
                                 s5w32, d_skip)
    pad_rows = lambda a: jnp.zeros((tb, a.shape[1]), a.dtype).at[:ns].set(a)
    h, xn, logits = _mix(pad_rows(attn_s), pad_rows(z_s), g_s, xs_pad, mix_w32, n_tot, n_p // tb, prev=outs)

    sf, blk, counts = _route(logits)
    n_tiles = (n_tot * TOP_K) // MOE_TILE + N_EXPERTS
    info, pad, tile_expert, tile_valid = _moe_plan(blk, counts, n_tiles)
    xs = _dispatch(info, pad, sf, xn, n_tiles * MOE_TILE)
    ys = _experts(tile_expert, tile_valid, xs, wu, bu, wd, bd)
    y_p, y_s = _combine(info, sf, h, row2(norm_final), ys, n_p)

    heads = (N_HEADS, HEAD_DIM)
    state = (N_GROUPS, SSM_STATE)
    time_major = lambda a, b_, t_: a.reshape(b_, N_HEADS, HEAD_DIM, t_).transpose(0, 3, 1, 2)[None]
    return (y_p.reshape(nb, nt, D_MODEL), y_s[:ns].reshape(ns, 1, D_MODEL),
            time_major(kt, nb, nt), time_major(vt, nb, nt),
            hre_p.reshape(1, nb, *state), him_p.reshape(1, nb, *state),
            time_major(kt_s[:, :, :ns], 1, ns).reshape(1, ns, 1, *heads),
            time_major(vt_s[:, :, :ns], 1, ns).reshape(1, ns, 1, *heads),
            hre_s.reshape(1, ns, *state), him_s.reshape(1, ns, *state))


def kernel(x_prompt, x_sample, cache_k, cache_v, state_ssm_re, state_ssm_im, page_table, norm_mix, w_in, sb_bias, w_attn_up, ssm_lambda_re, ssm_lambda_im, ssm_log_dt, ssm_b_re, ssm_b_im, ssm_c_re, ssm_c_im, ssm_d, glu_w1, glu_b1, glu_w2, glu_b2, w_out, norm_ffn, router_w, router_b, moe_w_up, moe_b_up, moe_w_down, moe_b_down, norm_final):
    return _forward(x_prompt, x_sample, cache_k[0], cache_v[0], state_ssm_re[0], state_ssm_im[0], page_table,
                    norm_mix[0], w_in[0], sb_bias[0], w_attn_up[0], ssm_lambda_re[0], ssm_lambda_im[0],
                    ssm_log_dt[0], ssm_b_re[0], ssm_b_im[0], ssm_c_re[0], ssm_c_im[0], ssm_d[0],
                    glu_w1[0], glu_b1[0], glu_w2[0], glu_b2[0], w_out[0], norm_ffn[0], router_w[0],
                    router_b[0], moe_w_up[0], moe_b_up[0], moe_w_down[0], moe_b_down[0], norm_final)
```

```python
import functools
import math

import jax
import jax.numpy as jnp
from jax import lax
from jax.experimental import pallas as pl
from jax.experimental.pallas import tpu as pltpu

F32 = jnp.float32
BF16 = jnp.bfloat16

D_MODEL = 1024
N_HEADS = 8
HEAD_DIM = 64
ATTN_WIDTH = N_HEADS * HEAD_DIM
SSM_WIDTH = 512
SSM_GROUP = 16
N_GROUPS = SSM_WIDTH // SSM_GROUP
SSM_STATE = 64
SSM_FLAT = N_GROUPS * SSM_STATE
N_EXPERTS = 32
TOP_K = 4
D_FF = 1024
SWIGLU_LIMIT = 7.0
SWIGLU_ALPHA = 1.702
RMS_EPS = 1e-5
PAGE_SIZE = 128
IN_WIDTH = 3 * ATTN_WIDTH + SSM_WIDTH + 2 * D_MODEL

LANES = 128
ROW_BLOCK = 256
ATTN_BLOCK = 256
HEADS_PER_STEP = 8
S5_CHUNKS = SSM_WIDTH // LANES
S5_TIME_BLOCK = 128
MOE_TILE = 512
MOE_FF_CHUNK = 256
PAGES_PER_STEP = 16
NEG_BIG = -1e30
VMEM_LIMIT = 56 * 1024 * 1024


def _dot(a, b):
    return jnp.dot(a, b, preferred_element_type=F32)


def _dot_nt(a, b):
    return lax.dot_general(a, b, (((1,), (1,)), ((), ())), preferred_element_type=F32)


def _dot_hp(a, b):
    return jnp.dot(a, b, preferred_element_type=F32, precision=lax.Precision.HIGHEST)


def _dot_nt_hp(a, b):
    return lax.dot_general(a, b, (((1,), (1,)), ((), ())), preferred_element_type=F32,
                           precision=lax.Precision.HIGHEST)


def _softplus(z):
    return jnp.maximum(z, 0.0) + jnp.log(1.0 + jnp.exp2(jnp.abs(z) * (-math.log2(math.e))))


def _split_bf16(x):
    hi = x.astype(BF16)
    lo = (x - hi.astype(F32)).astype(BF16)
    return hi, lo


def _params(sem, vmem=VMEM_LIMIT):
    return pltpu.CompilerParams(dimension_semantics=sem, vmem_limit_bytes=vmem)


def _inproj_body(x_ref, nw_ref, wq_ref, wkv_ref, wug_ref, q_ref, kt_ref, vt_ref, ktb_ref, vtb_ref, u_ref, g_ref,
                 *, precise):
    dot, dot_nt = (_dot_hp, _dot_nt_hp) if precise else (_dot, _dot_nt)
    x = x_ref[...]
    ms = jnp.mean(x * x, axis=-1, keepdims=True)
    xn = x * lax.rsqrt(ms + RMS_EPS) * nw_ref[...]
    if not precise:
        xn = xn.astype(BF16)
    a = ATTN_WIDTH
    q_ref[...] = (dot(xn, wq_ref[...]) * (HEAD_DIM ** -0.5)).astype(BF16)
    kvt = dot_nt(wkv_ref[...], xn)
    kt_ref[0] = kvt[:a]
    ktb_ref[0] = kvt[:a].astype(BF16)
    vt_ref[0] = kvt[a:]
    vtb_ref[0] = kvt[a:].astype(BF16)
    ug = dot(xn, wug_ref[...])
    u_ref[...] = ug[:, :SSM_WIDTH]
    g_ref[...] = ug[:, SSM_WIDTH:]


def _inproj(x2d, norm_w, w_in_bf, nb):
    n = x2d.shape[0]
    nt = n // nb
    tm = ROW_BLOCK
    nq = nt // tm
    a = ATTN_WIDTH
    row = lambda b, i: (b * nq + i, 0)
    full = lambda b, i: (0, 0)
    tmaj = lambda b, i: (b, 0, i)
    wq = w_in_bf[:, :a]
    wkv_t = w_in_bf[:, a:3 * a].T
    wug = w_in_bf[:, 3 * a:]
    outs = [
        jax.ShapeDtypeStruct((n, a), BF16),
        jax.ShapeDtypeStruct((nb, a, nt), F32),
        jax.ShapeDtypeStruct((nb, a, nt), F32),
        jax.ShapeDtypeStruct((nb, a, nt), BF16),
        jax.ShapeDtypeStruct((nb, a, nt), BF16),
        jax.ShapeDtypeStruct((n, SSM_WIDTH), F32),
        jax.ShapeDtypeStruct((n, 2 * D_MODEL), F32),
    ]
    return pl.pallas_call(
        functools.partial(_inproj_body, precise=w_in_bf.dtype == F32),
        out_shape=outs,
        grid=(nb, nq),
        in_specs=[pl.BlockSpec((tm, D_MODEL), row),
                  pl.BlockSpec((1, D_MODEL), full),
                  pl.BlockSpec(wq.shape, full), pl.BlockSpec(wkv_t.shape, full), pl.BlockSpec(wug.shape, full)],
        out_specs=[pl.BlockSpec((tm, a), row)] + [pl.BlockSpec((1, a, tm), tmaj)] * 4
                  + [pl.BlockSpec((tm, SSM_WIDTH), row), pl.BlockSpec((tm, 2 * D_MODEL), row)],
        compiler_params=_params(("parallel", "parallel")),
        name="inproj",
    )(x2d, norm_w, wq, wkv_t, wug)


def _attn_body(bias_ref, q_ref, k_ref, v_ref, o_ref, acc_ref):
    tq = tk = ATTN_BLOCK
    hg = pl.program_id(1)
    qi = pl.program_id(2)
    nh = HEADS_PER_STEP
    r = lax.broadcasted_iota(jnp.int32, (tq, tk), 0)
    c = lax.broadcasted_iota(jnp.int32, (tq, tk), 1)
    later = (r > c).astype(BF16)
    causal = c < r
    rows = [slice(h * HEAD_DIM, (h + 1) * HEAD_DIM) for h in range(nh)]
    extra = 16
    col = lax.broadcasted_iota(jnp.int32, (tq, extra), 1)
    ones2 = jnp.where(col < 2, 1.0, 0.0).astype(BF16)
    qh = [jnp.concatenate([q_ref[:, rows[h]], ones2], axis=1) for h in range(nh)]
    rowi = lax.broadcasted_iota(jnp.int32, (extra, tk), 0)
    bias_rows = []
    for h in range(nh):
        b = jnp.full((extra, tk), bias_ref[hg * nh + h], F32)
        b_hi = b.astype(BF16).astype(F32)
        bias_rows.append(jnp.where(rowi == 0, b_hi, jnp.where(rowi == 1, b - b_hi, 0.0)).astype(BF16))
    acc_ref[...] = jnp.zeros_like(acc_ref)

    def block(kb, carries, diagonal):
        start = pl.multiple_of(kb * tk, tk)
        kblk = k_ref[0, :, pl.ds(start, tk)]
        vblk = v_ref[0, :, pl.ds(start, tk)]
        zs = [_dot(qh[h], jnp.concatenate([kblk[rows[h], :], bias_rows[h]], axis=0)) for h in range(nh)]
        sps = [_softplus(z) for z in zs]
        if diagonal:
            sps = [jnp.where(causal, sp, 0.0) for sp in sps]
        newers = []
        for sp in sps:
            hi, lo = _split_bf16(sp)
            newers.append(_dot(hi, later) + _dot(lo, later))
        new = []
        for h in range(nh):
            w = jnp.exp(zs[h] - sps[h] - newers[h])
            if diagonal:
                w = jnp.where(causal, w, 0.0)
            pv = _dot_nt(w.astype(BF16), vblk[rows[h], :])
            acc_ref[h] += pv if diagonal else jnp.exp(-carries[h]) * pv
            new.append(carries[h] + newers[h][:, 0:1] + sps[h][:, 0:1])
        return tuple(new)

    zero = tuple(jnp.zeros((tq, 1), F32) for _ in range(nh))
    carries = block(qi, zero, True)
    odd = qi % 2
    carries = lax.cond(odd == 1, lambda cs: block(qi - 1, cs, False), lambda cs: cs, carries)
    top = qi - 1 - odd

    def pair(j, cs):
        return block(top - 2 * j - 1, block(top - 2 * j, cs, False), False)

    lax.fori_loop(0, qi // 2, pair, carries)
    for h in range(nh):
        o_ref[:, rows[h]] = acc_ref[h].astype(o_ref.dtype)


def _attn_prompt(q, ktb, vtb, bias, nb, nt):
    tq = ATTN_BLOCK
    nq = nt // tq
    width = HEADS_PER_STEP * HEAD_DIM
    ng = ATTN_WIDTH // width
    return pl.pallas_call(
        _attn_body,
        out_shape=jax.ShapeDtypeStruct((nb * nt, ATTN_WIDTH), BF16),
        grid=(nb, ng, nq),
        in_specs=[pl.BlockSpec(memory_space=pltpu.SMEM),
                  pl.BlockSpec((tq, width), lambda b, g, i: (b * nq + i, g)),
                  pl.BlockSpec((1, width, nt), lambda b, g, i: (b, g, 0)),
                  pl.BlockSpec((1, width, nt), lambda b, g, i: (b, g, 0))],
        out_specs=pl.BlockSpec((tq, width), lambda b, g, i: (b * nq + i, g)),
        scratch_shapes=[pltpu.VMEM((HEADS_PER_STEP, tq, HEAD_DIM), F32)],
        compiler_params=_params(("parallel", "parallel", "arbitrary")),
        name="attn_prompt",
    )(bias, q, ktb, vtb)


def _attn_sample_body(pt_ref, bias_ref, q_ref, *refs):
    np_ = PAGES_PER_STEP
    k_refs = refs[:np_]
    v_refs = refs[np_:2 * np_]
    o_ref, carry_ref, acc_ref = refs[2 * np_:]
    c = pl.program_id(1)

    @pl.when(c == 0)
    def _():
        carry_ref[...] = jnp.zeros_like(carry_ref)
        acc_ref[...] = jnp.zeros_like(acc_ref)

    head_of_lane = lax.broadcasted_iota(jnp.int32, (N_HEADS, ATTN_WIDTH), 1) // HEAD_DIM
    head_of_row = lax.broadcasted_iota(jnp.int32, (N_HEADS, ATTN_WIDTH), 0)
    own = head_of_lane == head_of_row
    qrow = q_ref[0].astype(F32)
    qbd = jnp.where(own, jnp.broadcast_to(qrow, (N_HEADS, ATTN_WIDTH)), 0.0).astype(BF16)
    r = lax.broadcasted_iota(jnp.int32, (PAGE_SIZE, PAGE_SIZE), 0)
    cc = lax.broadcasted_iota(jnp.int32, (PAGE_SIZE, PAGE_SIZE), 1)
    later = (r > cc).astype(BF16)
    bias = bias_ref[...]
    kcat = jnp.concatenate([k_refs[i][0].astype(BF16) for i in range(np_)], axis=1)
    vcat = jnp.concatenate([v_refs[i][0].astype(BF16) for i in range(np_)], axis=1)
    z = _dot(qbd, kcat) + bias
    sp = _softplus(z)
    page = lambda a, i: a[:, i * PAGE_SIZE:(i + 1) * PAGE_SIZE]
    sp_rows = jnp.concatenate([page(sp, i) for i in range(np_)], axis=0)
    hi, lo = _split_bf16(sp_rows)
    newer_rows = _dot(hi, later) + _dot(lo, later)
    carry = carry_ref[...]
    ws = []
    for i in range(np_):
        sp_i = page(sp, i)
        newer_i = newer_rows[i * N_HEADS:(i + 1) * N_HEADS, :]
        ws.append(jnp.exp(page(z, i) - sp_i - newer_i - carry))
        carry = carry + newer_i[:, 0:1] + sp_i[:, 0:1]
    carry_ref[...] = carry
    w = jnp.concatenate(ws, axis=1).astype(BF16)
    acc = acc_ref[...] + _dot_nt(w, vcat)
    acc_ref[...] = acc

    @pl.when(c == pl.num_programs(1) - 1)
    def _():
        o_ref[0] = jnp.sum(jnp.where(own, acc, 0.0), axis=0, keepdims=True).astype(o_ref.dtype)


def _attn_sample(q, cache_k, cache_v, page_table, bias):
    nseq, n_pages = page_table.shape
    np_ = PAGES_PER_STEP
    nchunk = n_pages // np_

    def page_map(i):
        def f(b, c, pt):
            return (pt[b * n_pages + (n_pages - 1 - (c * np_ + i))], 0, 0)
        return f

    page_spec = [pl.BlockSpec((1, ATTN_WIDTH, PAGE_SIZE), page_map(i)) for i in range(np_)]
    grid_spec = pltpu.PrefetchScalarGridSpec(
        num_scalar_prefetch=1,
        grid=(nseq, nchunk),
        in_specs=[pl.BlockSpec((N_HEADS, 1), lambda b, c, pt: (0, 0)),
                  pl.BlockSpec((1, 1, ATTN_WIDTH), lambda b, c, pt: (b, 0, 0))]
                 + page_spec + page_spec,
        out_specs=pl.BlockSpec((1, 1, ATTN_WIDTH), lambda b, c, pt: (b, 0, 0)),
        scratch_shapes=[pltpu.VMEM((N_HEADS, 1), F32), pltpu.VMEM((N_HEADS, ATTN_WIDTH), F32)],
    )
    out = pl.pallas_call(
        _attn_sample_body,
        out_shape=jax.ShapeDtypeStruct((nseq, 1, ATTN_WIDTH), BF16),
        grid_spec=grid_spec,
        compiler_params=_params(("parallel", "arbitrary")),
        name="attn_sample",
    )(page_table.reshape(-1), bias.reshape(N_HEADS, 1), q.reshape(nseq, 1, ATTN_WIDTH),
      *([cache_k] * np_), *([cache_v] * np_))
    return out.reshape(nseq, ATTN_WIDTH)


def _s5_prep_body(lr_ref, li_ref, ldt_ref, br_ref, bi_ref, ar_ref, ai_ref, bbr_ref, bbi_ref):
    lr = lr_ref[...]
    li = li_ref[...]
    dt = jnp.exp(ldt_ref[...])
    mag = jnp.exp(lr * dt)
    ar = mag * jnp.cos(li * dt)
    ai = mag * jnp.sin(li * dt)
    den = lr * lr + li * li
    nr = ar - 1.0
    ni = ai
    fr = (nr * lr + ni * li) / den
    fi = (ni * lr - nr * li) / den
    ar_ref[...] = ar
    ai_ref[...] = ai
    br = br_ref[...]
    bi = bi_ref[...]
    bbr_ref[...] = fr[:, None, :] * br - fi[:, None, :] * bi
    bbi_ref[...] = fr[:, None, :] * bi + fi[:, None, :] * br


def _s5_prep(lam_re, lam_im, log_dt, b_re, b_im):
    g, p, h = N_GROUPS, SSM_STATE, SSM_GROUP
    outs = [jax.ShapeDtypeStruct((g, p), F32)] * 2 + [jax.ShapeDtypeStruct((g, h, p), F32)] * 2
    return pl.pallas_call(_s5_prep_body, out_shape=outs, name="s5_prep")(
        lam_re, lam_im, log_dt.reshape(g, 1),
        jnp.swapaxes(b_re, 1, 2), jnp.swapaxes(b_im, 1, 2))


def _s5_weights(ar, ai, bbr_t, bbi_t, c_re, c_im, dtype):
    eye = jnp.eye(8, dtype=F32)

    def drive(bb_t):
        x = bb_t.reshape(S5_CHUNKS, 8, SSM_GROUP, SSM_STATE)
        return (x[:, :, :, None, :] * eye[None, :, None, :, None]).reshape(
            S5_CHUNKS, LANES, 8 * SSM_STATE).astype(dtype)

    def readout(cm):
        x = jnp.swapaxes(cm, 1, 2).reshape(S5_CHUNKS, 8, SSM_STATE, SSM_GROUP)
        return (x[:, :, :, None, :] * eye[None, :, None, :, None]).reshape(
            S5_CHUNKS, 8 * SSM_STATE, LANES).astype(dtype)

    return (ar.reshape(1, SSM_FLAT), ai.reshape(1, SSM_FLAT), drive(bbr_t), drive(bbi_t),
            readout(c_re), readout(-c_im))


def _gelu_tanh(x):
    return 0.5 * x * (1.0 + jnp.tanh(math.sqrt(2.0 / math.pi) * (x + 0.044715 * (x * x * x))))


def _s5_prompt_body(u_ref, ar_ref, ai_ref, bbr_ref, bbi_ref, cr_ref, nci_ref, d_ref,
                    z_ref, hre_ref, him_ref, hs_ref, st_ref, *, nb, tt, cw):
    i = pl.program_id(0)
    cs = 8 * SSM_STATE
    tiles_per_chunk = cs // LANES
    im0 = SSM_FLAT // LANES

    @pl.when(i == 0)
    def _():
        st_ref[...] = jnp.zeros_like(st_ref)

    for b in range(nb):
        rows = pl.ds(b, tt, stride=nb)
        for c in range(S5_CHUNKS):
            uc = u_ref[b, :, c * LANES:(c + 1) * LANES].astype(BF16)
            dre = _dot(uc, bbr_ref[c])
            dim = _dot(uc, bbi_ref[c])
            for m in range(tiles_per_chunk):
                lanes = slice(m * LANES, (m + 1) * LANES)
                hs_ref[c * tiles_per_chunk + m, rows, :] = dre[:, lanes]
                hs_ref[im0 + c * tiles_per_chunk + m, rows, :] = dim[:, lanes]

    nt_scan = cw // LANES
    for j in range(SSM_FLAT // cw):
        cols = slice(j * cw, (j + 1) * cw)
        ar = [jnp.broadcast_to(ar_ref[:, j * cw + m * LANES:j * cw + (m + 1) * LANES], (nb, LANES))
              for m in range(nt_scan)]
        ai = [jnp.broadcast_to(ai_ref[:, j * cw + m * LANES:j * cw + (m + 1) * LANES], (nb, LANES))
              for m in range(nt_scan)]

        def step(t, carry):
            rows = pl.ds(pl.multiple_of(t * nb, nb), nb)
            out = []
            for m in range(nt_scan):
                hr, hi = carry[2 * m], carry[2 * m + 1]
                tre = j * nt_scan + m
                nr = ar[m] * hr - ai[m] * hi + hs_ref[tre, rows, :]
                ni = ar[m] * hi + ai[m] * hr + hs_ref[im0 + tre, rows, :]
                hs_ref[tre, rows, :] = nr
                hs_ref[im0 + tre, rows, :] = ni
                out += [nr, ni]
            return tuple(out)

        init = []
        for m in range(nt_scan):
            lanes = slice(j * cw + m * LANES, j * cw + (m + 1) * LANES)
            init += [st_ref[0, :, lanes], st_ref[1, :, lanes]]
        fin = lax.fori_loop(0, tt, step, tuple(init), unroll=8)
        for m in range(nt_scan):
            lanes = slice(j * cw + m * LANES, j * cw + (m + 1) * LANES)
            st_ref[0, :, lanes] = fin[2 * m]
            st_ref[1, :, lanes] = fin[2 * m + 1]

    for b in range(nb):
        rows = pl.ds(b, tt, stride=nb)
        for c in range(S5_CHUNKS):
            t0 = c * tiles_per_chunk
            hre = jnp.concatenate([hs_ref[t0 + m, rows, :] for m in range(tiles_per_chunk)], axis=-1)
            him = jnp.concatenate([hs_ref[im0 + t0 + m, rows, :] for m in range(tiles_per_chunk)], axis=-1)
            lanes = slice(c * LANES, (c + 1) * LANES)
            y = (_dot(hre.astype(BF16), cr_ref[c]) + _dot(him.astype(BF16), nci_ref[c])
                 + d_ref[:, lanes] * u_ref[b, :, lanes])
            z_ref[b, :, lanes] = _gelu_tanh(y).astype(z_ref.dtype)

    @pl.when(i == pl.num_programs(0) - 1)
    def _():
        hre_ref[...] = st_ref[0]
        him_ref[...] = st_ref[1]


def _s5_prompt(u, weights, d_skip):
    nb, nt, _ = u.shape
    tt = min(S5_TIME_BLOCK, nt)
    ar, ai, bbr, bbi, cr, nci = weights
    full2 = lambda i: (0, 0)
    full3 = lambda i: (0, 0, 0)
    cs = 8 * SSM_STATE
    body = functools.partial(_s5_prompt_body, nb=nb, tt=tt, cw=512)
    return pl.pallas_call(
        body,
        out_shape=[jax.ShapeDtypeStruct((nb, nt, SSM_WIDTH), BF16),
                   jax.ShapeDtypeStruct((nb, SSM_FLAT), F32),
                   jax.ShapeDtypeStruct((nb, SSM_FLAT), F32)],
        grid=(nt // tt,),
        in_specs=[pl.BlockSpec((nb, tt, SSM_WIDTH), lambda i: (0, i, 0)),
                  pl.BlockSpec((1, SSM_FLAT), full2), pl.BlockSpec((1, SSM_FLAT), full2),
                  pl.BlockSpec((S5_CHUNKS, LANES, cs), full3), pl.BlockSpec((S5_CHUNKS, LANES, cs), full3),
                  pl.BlockSpec((S5_CHUNKS, cs, LANES), full3), pl.BlockSpec((S5_CHUNKS, cs, LANES), full3),
                  pl.BlockSpec((1, SSM_WIDTH), full2)],
        out_specs=[pl.BlockSpec((nb, tt, SSM_WIDTH), lambda i: (0, i, 0)),
                   pl.BlockSpec((nb, SSM_FLAT), full2), pl.BlockSpec((nb, SSM_FLAT), full2)],
        scratch_shapes=[pltpu.VMEM((2 * SSM_FLAT // LANES, tt * nb, LANES), F32),
                        pltpu.VMEM((2, nb, SSM_FLAT), F32)],
        compiler_params=_params(("arbitrary",)),
        name="s5_prompt",
    )(u, ar, ai, bbr, bbi, cr, nci, d_skip.reshape(1, SSM_WIDTH))


def _s5_step_body(u_ref, h0r_ref, h0i_ref, ar_ref, ai_ref, bbr_ref, bbi_ref, cr_ref, nci_ref, d_ref,
                  z_ref, hre_ref, him_ref):
    cs = 8 * SSM_STATE
    u = u_ref[...]
    for c in range(S5_CHUNKS):
        cols = slice(c * cs, (c + 1) * cs)
        lanes = slice(c * LANES, (c + 1) * LANES)
        uc = u[:, lanes]
        ar = ar_ref[:, cols]
        ai = ai_ref[:, cols]
        h0r = h0r_ref[:, cols]
        h0i = h0i_ref[:, cols]
        hr = ar * h0r - ai * h0i + _dot_hp(uc, bbr_ref[c])
        hi = ar * h0i + ai * h0r + _dot_hp(uc, bbi_ref[c])
        hre_ref[:, cols] = hr
        him_ref[:, cols] = hi
        y = _dot_hp(hr, cr_ref[c]) + _dot_hp(hi, nci_ref[c]) + d_ref[:, lanes] * uc
        z_ref[:, lanes] = _gelu_tanh(y)


def _s5_step(u, h0_re, h0_im, weights, d_skip):
    n = u.shape[0]
    ar, ai, bbr, bbi, cr, nci = weights
    return pl.pallas_call(
        _s5_step_body,
        out_shape=[jax.ShapeDtypeStruct((n, SSM_WIDTH), F32),
                   jax.ShapeDtypeStruct((n, SSM_FLAT), F32),
                   jax.ShapeDtypeStruct((n, SSM_FLAT), F32)],
        name="s5_step",
    )(u, h0_re, h0_im, ar, ai, bbr, bbi, cr, nci, d_skip.reshape(1, SSM_WIDTH))


def _mix_body(attn_ref, z_ref, g_ref, x_ref, wup_ref, w1_ref, b1_ref, w2_ref, b2_ref, wout_ref,
              nf_ref, rw_ref, rb_ref, *rest, n_blocks, precise):
    h_ref, xn_ref, lg_ref = rest[-3:]

    @pl.when(pl.program_id(0) >= n_blocks)
    def _():
        h_ref[...] = jnp.zeros_like(h_ref)
        xn_ref[...] = jnp.zeros_like(xn_ref)
        lg_ref[...] = jnp.zeros_like(lg_ref)

    pl.when(pl.program_id(0) < n_blocks)(functools.partial(
        _mix_rows, attn_ref, z_ref, g_ref, x_ref, wup_ref, w1_ref, b1_ref, w2_ref, b2_ref, wout_ref,
        nf_ref, rw_ref, rb_ref, h_ref, xn_ref, lg_ref, precise=precise))


def _mix_rows(attn_ref, z_ref, g_ref, x_ref, wup_ref, w1_ref, b1_ref, w2_ref, b2_ref, wout_ref,
              nf_ref, rw_ref, rb_ref, h_ref, xn_ref, lg_ref, *, precise):
    dot = _dot_hp if precise else _dot
    operand = (lambda a: a.astype(F32)) if precise else (lambda a: a.astype(BF16))
    y_attn = dot(operand(attn_ref[...]), wup_ref[...])
    z = operand(z_ref[...])
    y_ssm = (dot(z, w1_ref[...]) + b1_ref[...]) * jax.nn.sigmoid(dot(z, w2_ref[...]) + b2_ref[...])
    g = g_ref[...]
    merged = jax.nn.sigmoid(g[:, :D_MODEL]) * y_attn + jax.nn.sigmoid(g[:, D_MODEL:]) * y_ssm
    h = x_ref[...] + dot(operand(merged), wout_ref[...])
    h_ref[...] = h
    ms = jnp.mean(h * h, axis=-1, keepdims=True)
    xn = h * lax.rsqrt(ms + RMS_EPS) * nf_ref[...]
    xn_ref[...] = xn
    lg_ref[...] = dot(operand(xn), rw_ref[...]) + rb_ref[...]


def _mix(attn, z, g, x, w, n_total, first_block, prev=None):
    n = x.shape[0]
    tm = ROW_BLOCK
    n_blocks = n // tm
    n_steps = n_blocks if prev is not None else n_total // tm - first_block
    row = lambda i: (jnp.minimum(i, n_blocks - 1), 0)
    orow = lambda i: (i + first_block, 0)
    full = lambda i: (0, 0)
    wspecs = [pl.BlockSpec(a.shape, full) for a in w]
    in_specs = [pl.BlockSpec((tm, ATTN_WIDTH), row), pl.BlockSpec((tm, SSM_WIDTH), row),
                pl.BlockSpec((tm, 2 * D_MODEL), row), pl.BlockSpec((tm, D_MODEL), row)] + wspecs
    args = [attn, z, g, x, *w]
    aliases = {}
    if prev is not None:
        base = len(args)
        in_specs = in_specs + [pl.BlockSpec(memory_space=pl.ANY)] * 3
        args = args + list(prev)
        aliases = {base: 0, base + 1: 1, base + 2: 2}
    return pl.pallas_call(
        functools.partial(_mix_body, n_blocks=n_blocks, precise=w[0].dtype == F32),
        out_shape=[jax.ShapeDtypeStruct((n_total, D_MODEL), F32),
                   jax.ShapeDtypeStruct((n_total, D_MODEL), F32),
                   jax.ShapeDtypeStruct((n_total, LANES), F32)],
        grid=(n_steps,),
        in_specs=in_specs,
        out_specs=[pl.BlockSpec((tm, D_MODEL), orow), pl.BlockSpec((tm, D_MODEL), orow),
                   pl.BlockSpec((tm, LANES), orow)],
        input_output_aliases=aliases,
        compiler_params=_params(("parallel",)),
        name="mix",
    )(*args)


def _route_body(lg_ref, sf_ref, blk_ref, cnt_ref, run_ref):
    tb = ROW_BLOCK
    i = pl.program_id(0)

    @pl.when(i == 0)
    def _():
        run_ref[...] = jnp.zeros_like(run_ref)

    lg = lg_ref[...]
    lane = lax.broadcasted_iota(jnp.int32, (tb, LANES), 1).astype(F32)
    work = lg
    tops, hots, idxs = [], [], []
    for _ in range(TOP_K):
        m = jnp.max(work, axis=-1, keepdims=True)
        idx = jnp.min(jnp.where(work == m, lane, float(LANES)), axis=-1, keepdims=True)
        hot = lane == idx
        work = jnp.where(hot, -jnp.inf, work)
        tops.append(m)
        hots.append(hot)
        idxs.append(idx)
    es = [jnp.exp(m - tops[0]) for m in tops]
    den = es[0] + es[1] + es[2] + es[3]
    sel = jnp.zeros((tb, LANES), F32)
    for hot in hots:
        sel = jnp.where(hot, 1.0, sel)
    r = lax.broadcasted_iota(jnp.int32, (tb, tb), 0)
    c = lax.broadcasted_iota(jnp.int32, (tb, tb), 1)
    earlier = (c < r).astype(BF16)
    sel_bf = sel.astype(BF16)
    lrank = _dot(earlier, sel_bf)
    lcnt = jnp.sum(sel, axis=0, keepdims=True)
    below = (lax.broadcasted_iota(jnp.int32, (LANES, LANES), 0)
             < lax.broadcasted_iota(jnp.int32, (LANES, LANES), 1)).astype(BF16)
    loff = _dot(jnp.broadcast_to(lcnt, (16, LANES)).astype(BF16), below)[0:1]
    lpos = lrank + loff
    sf = jnp.zeros((tb, LANES), F32)
    for k in range(TOP_K):
        pk = jnp.sum(jnp.where(hots[k], lpos, 0.0), axis=-1, keepdims=True)
        sf = jnp.where(lane == float(k), es[k] / den, sf)
        sf = jnp.where(lane == float(TOP_K + k), pk, sf)
    sf_ref[...] = sf
    blk_ref[0] = jnp.concatenate([lcnt, run_ref[...]] + [jnp.zeros((1, LANES), F32)] * 6, axis=0)
    run = run_ref[...] + lcnt
    run_ref[...] = run
    cnt_ref[...] = run


def _route(logits):
    n = logits.shape[0]
    tb = ROW_BLOCK
    row = lambda i: (i, 0)
    return pl.pallas_call(
        _route_body,
        out_shape=[jax.ShapeDtypeStruct((n, LANES), F32),
                   jax.ShapeDtypeStruct((n // tb, 8, LANES), F32),
                   jax.ShapeDtypeStruct((1, LANES), F32)],
        grid=(n // tb,),
        in_specs=[pl.BlockSpec((tb, LANES), row)],
        out_specs=[pl.BlockSpec((tb, LANES), row), pl.BlockSpec((1, 8, LANES), lambda i: (i, 0, 0)),
                   pl.BlockSpec((1, LANES), lambda i: (0, 0))],
        scratch_shapes=[pltpu.VMEM((1, LANES), F32)],
        compiler_params=_params(("arbitrary",)),
        name="route",
    )(logits)


ROW_TILE = D_MODEL // LANES
PAIRS = ROW_BLOCK * TOP_K
SEG_BITS = tuple(1 << b for b in range(ROW_BLOCK.bit_length() - 1, -1, -1))
PAD_BITS = tuple(1 << b for b in range(MOE_TILE.bit_length() - 2, -1, -1))


def _segment_copies(length, bits, make_copy):
    for bit in bits:
        @pl.when((length & bit) != 0)
        def _(bit=bit):
            make_copy(length & ~(2 * bit - 1), bit)


def _tile_rows(ref, first_row, n_rows):
    start = first_row * ROW_TILE
    if not isinstance(start, int):
        start = pl.multiple_of(start, ROW_TILE)
    return ref.at[pl.ds(start, n_rows * ROW_TILE)]


def _to_row_tiles(ref, value):
    n = value.shape[0]
    for s in range(ROW_TILE):
        ref[pl.ds(s, n, stride=ROW_TILE), :] = value[:, s * LANES:(s + 1) * LANES]


def _from_row_tiles(ref, n):
    return jnp.concatenate([ref[pl.ds(s, n, stride=ROW_TILE), :] for s in range(ROW_TILE)], axis=1)


def _dispatch_body(info_ref, pad_ref, sf_ref, x_ref, xs_ref, xl0_ref, xl1_ref, sem0, sem1):
    tb = ROW_BLOCK
    i = pl.program_id(0)

    @pl.when(i == 0)
    def _():
        xl1_ref[...] = jnp.zeros_like(xl1_ref)

        def fill(e, _):
            start = pad_ref[e]

            def copy(off, size):
                cp = pltpu.make_async_copy(_tile_rows(xl1_ref, 0, size), _tile_rows(xs_ref, start + off, size), sem1)
                cp.start()
                cp.wait()

            _segment_copies(pad_ref[N_EXPERTS + e], PAD_BITS, copy)
            return 0

        lax.fori_loop(0, N_EXPERTS, fill, 0)

        def clear_tile(t, _):
            cp = pltpu.make_async_copy(_tile_rows(xl1_ref, 0, MOE_TILE), _tile_rows(xs_ref, t * MOE_TILE, MOE_TILE), sem1)
            cp.start()
            cp.wait()
            return 0

        lax.fori_loop(pad_ref[2 * N_EXPERTS], xs_ref.shape[0] // (MOE_TILE * ROW_TILE), clear_tile, 0)

    def all_landed(xl_ref, sem):
        pltpu.make_async_copy(xl_ref, _tile_rows(xs_ref, 0, PAIRS), sem).wait()

    def step(xl_ref, sem, other_ref, other_sem):
        lpos_t = jnp.transpose(sf_ref[...])
        slot = lax.broadcasted_iota(jnp.int32, (PAIRS, tb), 0).astype(F32)
        onehot = jnp.zeros((PAIRS, tb), F32)
        for k in range(TOP_K):
            onehot = onehot + jnp.where(slot == lpos_t[TOP_K + k:TOP_K + k + 1, :], 1.0, 0.0)
        _to_row_tiles(xl_ref, _dot(onehot.astype(BF16), x_ref[...].astype(BF16)))

        def segment(e, _):
            length, src, dst = info_ref[0, 0, e], info_ref[0, 0, N_EXPERTS + e], info_ref[0, 0, 2 * N_EXPERTS + e]
            _segment_copies(length, SEG_BITS, lambda off, size: pltpu.make_async_copy(
                _tile_rows(xl_ref, src + off, size), _tile_rows(xs_ref, dst + off, size), sem).start())
            return 0

        lax.fori_loop(0, N_EXPERTS, segment, 0)

        @pl.when(i > 0)
        def _():
            all_landed(other_ref, other_sem)

        @pl.when(i == pl.num_programs(0) - 1)
        def _():
            all_landed(xl_ref, sem)

    pl.when(i % 2 == 0)(functools.partial(step, xl0_ref, sem0, xl1_ref, sem1))
    pl.when(i % 2 == 1)(functools.partial(step, xl1_ref, sem1, xl0_ref, sem0))


def _dispatch(info, pad, sf, xn, n_sorted):
    n = xn.shape[0]
    tb = ROW_BLOCK
    row = lambda i: (i, 0)
    return pl.pallas_call(
        _dispatch_body,
        out_shape=jax.ShapeDtypeStruct((n_sorted * ROW_TILE, LANES), F32),
        grid=(n // tb,),
        in_specs=[pl.BlockSpec((1, 1, LANES), lambda i: (i, 0, 0), memory_space=pltpu.SMEM),
                  pl.BlockSpec(memory_space=pltpu.SMEM),
                  pl.BlockSpec((tb, LANES), row),
                  pl.BlockSpec((tb, D_MODEL), row)],
        out_specs=pl.BlockSpec(memory_space=pl.ANY),
        scratch_shapes=[pltpu.VMEM((PAIRS * ROW_TILE, LANES), F32), pltpu.VMEM((PAIRS * ROW_TILE, LANES), F32),
                        pltpu.SemaphoreType.DMA, pltpu.SemaphoreType.DMA],
        compiler_params=_params(("arbitrary",)),
        name="dispatch",
    )(info, pad, sf, xn)


def _combine_body(info_ref, next_ref, sf_ref, h_ref, nw_ref, ys_ref, yp_ref, ys_out_ref,
                  yl0_ref, yl1_ref, sem0, sem1, *, n_prompt_blocks):
    tb = ROW_BLOCK
    i = pl.program_id(0)

    def fetch(blk_ref, yl_ref, sem):
        def segment(e, _):
            length, dst, src = blk_ref[0, 0, e], blk_ref[0, 0, N_EXPERTS + e], blk_ref[0, 0, 2 * N_EXPERTS + e]
            _segment_copies(length, SEG_BITS, lambda off, size: pltpu.make_async_copy(
                _tile_rows(ys_ref, src + off, size), _tile_rows(yl_ref, dst + off, size), sem).start())
            return 0

        lax.fori_loop(0, N_EXPERTS, segment, 0)

    def step(yl_ref, sem, other_ref, other_sem):
        @pl.when(i == 0)
        def _():
            fetch(info_ref, yl_ref, sem)

        @pl.when(i + 1 < pl.num_programs(0))
        def _():
            fetch(next_ref, other_ref, other_sem)

        sf = sf_ref[...]
        pltpu.make_async_copy(_tile_rows(ys_ref, 0, PAIRS), yl_ref, sem).wait()
        out = h_ref[...]
        for c in range(TOP_K):
            slot = (lax.broadcasted_iota(jnp.int32, (tb, tb), 1) + c * tb).astype(F32)
            gates = jnp.zeros((tb, tb), F32)
            for k in range(TOP_K):
                gates = gates + jnp.where(slot == sf[:, TOP_K + k:TOP_K + k + 1], sf[:, k:k + 1], 0.0)
            g_hi, g_lo = _split_bf16(gates)
            y_hi, y_lo = _split_bf16(_from_row_tiles(_tile_rows(yl_ref, c * tb, tb), tb))
            out = out + (_dot(g_hi, y_hi) + _dot(g_lo, y_hi) + _dot(g_hi, y_lo))
        ms = jnp.mean(out * out, axis=-1, keepdims=True)
        y = out * lax.rsqrt(ms + RMS_EPS) * nw_ref[...]

        @pl.when(i < n_prompt_blocks)
        def _():
            yp_ref[...] = y

        @pl.when(i >= n_prompt_blocks)
        def _():
            ys_out_ref[...] = y

    pl.when(i % 2 == 0)(functools.partial(step, yl0_ref, sem0, yl1_ref, sem1))
    pl.when(i % 2 == 1)(functools.partial(step, yl1_ref, sem1, yl0_ref, sem0))


def _combine(info, sf, h, norm_w, ys, n_prompt):
    n = h.shape[0]
    tb = ROW_BLOCK
    nblk = n // tb
    npb = n_prompt // tb
    row = lambda i: (i, 0)
    return pl.pallas_call(
        functools.partial(_combine_body, n_prompt_blocks=npb),
        out_shape=[jax.ShapeDtypeStruct((n_prompt, D_MODEL), F32),
                   jax.ShapeDtypeStruct((n - n_prompt, D_MODEL), F32)],
        grid=(nblk,),
        in_specs=[pl.BlockSpec((1, 1, LANES), lambda i: (i, 0, 0), memory_space=pltpu.SMEM),
                  pl.BlockSpec((1, 1, LANES), lambda i: (jnp.minimum(i + 1, nblk - 1), 0, 0),
                               memory_space=pltpu.SMEM),
                  pl.BlockSpec((tb, LANES), row),
                  pl.BlockSpec((tb, D_MODEL), row),
                  pl.BlockSpec((1, D_MODEL), lambda i: (0, 0)),
                  pl.BlockSpec(memory_space=pl.ANY)],
        out_specs=[pl.BlockSpec((tb, D_MODEL), lambda i: (jnp.minimum(i, npb - 1), 0)),
                   pl.BlockSpec((tb, D_MODEL), lambda i: (jnp.maximum(i - npb, 0), 0))],
        scratch_shapes=[pltpu.VMEM((PAIRS * ROW_TILE, LANES), F32), pltpu.VMEM((PAIRS * ROW_TILE, LANES), F32),
                        pltpu.SemaphoreType.DMA, pltpu.SemaphoreType.DMA],
        compiler_params=_params(("arbitrary",)),
        name="combine",
    )(info, info, sf, h, norm_w, ys)


def _experts_body(te_ref, tv_ref, xs_ref, wu_ref, bu_ref, wd_ref, bd_ref, ys_ref, stage_ref, wdb_ref, wub_ref):
    i = pl.program_id(0)

    @pl.when((i == 0) | (te_ref[i] != te_ref[jnp.maximum(i - 1, 0)]))
    def _():
        wub_ref[...] = wu_ref[0].astype(BF16)
        half = D_FF // 2
        for c in range(ROW_TILE):
            lanes = slice(c * LANES, (c + 1) * LANES)
            stage_ref[c, pl.ds(0, half, stride=2), :] = wd_ref[0, :half, lanes]
            stage_ref[c, pl.ds(1, half, stride=2), :] = wd_ref[0, half:, lanes]
        wdb_ref[...] = jnp.concatenate([stage_ref[c] for c in range(ROW_TILE)], axis=1).astype(BF16)

    @pl.when(tv_ref[i] == 0)
    def _():
        ys_ref[...] = jnp.zeros_like(ys_ref)

    @pl.when(tv_ref[i] != 0)
    def _():
        x = _from_row_tiles(xs_ref, MOE_TILE).astype(BF16)
        even = lax.broadcasted_iota(jnp.int32, (MOE_TILE, LANES), 1) % 2 == 0
        acts = []
        for f in range(D_FF // MOE_FF_CHUNK):
            c1 = slice(f * MOE_FF_CHUNK, (f + 1) * MOE_FF_CHUNK)
            c2 = slice(D_FF + f * MOE_FF_CHUNK, D_FF + (f + 1) * MOE_FF_CHUNK)
            h1 = _dot(x, wub_ref[:, c1]) + bu_ref[0, :, c1]
            h2 = _dot(x, wub_ref[:, c2]) + bu_ref[0, :, c2]
            glu, lin = [], []
            for m in range(MOE_FF_CHUNK // LANES):
                a = h1[:, m * LANES:(m + 1) * LANES]
                b = h2[:, m * LANES:(m + 1) * LANES]
                glu.append(jnp.where(even, a, pltpu.roll(b, 1, 1)))
                lin.append(jnp.where(even, pltpu.roll(a, LANES - 1, 1), b))
            x_glu = jnp.minimum(jnp.concatenate(glu, axis=-1), SWIGLU_LIMIT)
            x_lin = jnp.clip(jnp.concatenate(lin, axis=-1), -SWIGLU_LIMIT, SWIGLU_LIMIT)
            acts.append((x_glu * jax.nn.sigmoid(SWIGLU_ALPHA * x_glu) * (x_lin + 1.0)).astype(BF16))
        _to_row_tiles(ys_ref, _dot(jnp.concatenate(acts, axis=-1), wdb_ref[...]) + bd_ref[0])


def _experts(tile_expert, tile_valid, xs, wu, bu, wd, bd):
    n_tiles = tile_expert.shape[0]
    tm = MOE_TILE
    wmap = lambda i, te, tv: (te[i], 0, 0)
    grid_spec = pltpu.PrefetchScalarGridSpec(
        num_scalar_prefetch=2,
        grid=(n_tiles,),
        in_specs=[pl.BlockSpec((tm * ROW_TILE, LANES), lambda i, te, tv: (i * tv[i], 0)),
                  pl.BlockSpec((1, D_MODEL, 2 * D_FF), wmap), pl.BlockSpec((1, 1, 2 * D_FF), wmap),
                  pl.BlockSpec((1, D_FF, D_MODEL), wmap), pl.BlockSpec((1, 1, D_MODEL), wmap)],
        out_specs=pl.BlockSpec((tm * ROW_TILE, LANES), lambda i, te, tv: (i, 0)),
        scratch_shapes=[pltpu.VMEM((ROW_TILE, D_FF, LANES), F32), pltpu.VMEM((D_FF, D_MODEL), BF16),
                        pltpu.VMEM((D_MODEL, 2 * D_FF), BF16)],
    )
    return pl.pallas_call(
        _experts_body,
        out_shape=jax.ShapeDtypeStruct((n_tiles * tm * ROW_TILE, LANES), F32),
        grid_spec=grid_spec,
        compiler_params=_params(("arbitrary",)),
        name="experts",
    )(tile_expert, tile_valid, xs, wu, bu, wd, bd)


def _moe_plan(blk, counts, n_tiles):
    as_int = lambda a: a.astype(jnp.int32)
    seg_len = as_int(blk[:, 0, :N_EXPERTS])
    before = as_int(blk[:, 1, :N_EXPERTS])
    cnt = as_int(counts[0, :N_EXPERTS])
    tiles = (cnt + MOE_TILE - 1) // MOE_TILE
    ends = jnp.cumsum(tiles)
    offs = (ends - tiles) * MOE_TILE
    local = jnp.cumsum(seg_len, axis=1) - seg_len
    info = jnp.concatenate([seg_len, local, offs[None, :] + before,
                            jnp.zeros((seg_len.shape[0], LANES - 3 * N_EXPERTS), jnp.int32)], axis=1)
    pad = jnp.concatenate([offs + cnt, tiles * MOE_TILE - cnt, ends[-1:]])
    tile_id = jnp.arange(n_tiles, dtype=jnp.int32)
    tile_expert = jnp.minimum(jnp.sum((tile_id[:, None] >= ends[None, :]).astype(jnp.int32), axis=1),
                              N_EXPERTS - 1)
    tile_valid = (tile_id < ends[-1]).astype(jnp.int32)
    return info[:, None, :], pad, tile_expert, tile_valid


def _forward(x_prompt, x_sample, cache_k, cache_v, state_re, state_im, page_table,
             norm_mix, w_in, sb_bias, w_attn_up, lam_re, lam_im, log_dt, b_re, b_im, c_re, c_im,
             d_skip, glu_w1, glu_b1, glu_w2, glu_b2, w_out, norm_ffn, router_w, router_b,
             moe_w_up, moe_b_up, moe_w_down, moe_b_down, norm_final):
    nb, nt, _ = x_prompt.shape
    ns = x_sample.shape[0]
    n_p = nb * nt
    tb = ROW_BLOCK
    n_tot = n_p + tb
    row2 = lambda a: a.reshape(1, -1)

    w_in_bf = w_in.astype(BF16)
    mix_w32 = [w_attn_up, glu_w1, row2(glu_b1), glu_w2, row2(glu_b2), w_out, row2(norm_ffn),
               jnp.pad(router_w, ((0, 0), (0, LANES - N_EXPERTS))),
               jnp.pad(row2(router_b), ((0, 0), (0, LANES - N_EXPERTS)), constant_values=NEG_BIG)]
    mix_w = [a.astype(BF16) if a.shape[0] > 1 else a for a in mix_w32]
    wu = moe_w_up
    bu = moe_b_up[:, None, :]
    wd = moe_w_down
    bd = moe_b_down[:, None, :]
    ar, ai, bbr_t, bbi_t = _s5_prep(lam_re, lam_im, log_dt, b_re, b_im)
    s5w = _s5_weights(ar, ai, bbr_t, bbi_t, c_re, c_im, BF16)
    s5w32 = _s5_weights(ar, ai, bbr_t, bbi_t, c_re, c_im, F32)

    xp = x_prompt.reshape(n_p, D_MODEL)
    q, kt, vt, ktb, vtb, u, g = _inproj(xp, row2(norm_mix), w_in_bf, nb)
    attn = _attn_prompt(q, ktb, vtb, sb_bias, nb, nt)
    z, hre_p, him_p = _s5_prompt(u.reshape(nb, nt, SSM_WIDTH), s5w, d_skip)
    outs = _mix(attn, z.reshape(n_p, SSM_WIDTH), g, xp, mix_w, n_tot, 0)

    xs_pad = jnp.zeros((tb, D_MODEL), F32).at[:ns].set(x_sample.reshape(ns, D_MODEL))
    q_s, kt_s, vt_s, _, _, u_s, g_s = _inproj(xs_pad, row2(norm_mix), w_in, 1)
    n_pool = cache_k.shape[0]
    pages = lambda c: c.transpose(0, 2, 3, 1).reshape(n_pool, ATTN_WIDTH, PAGE_SIZE)
    attn_s = _attn_sample(q_s[:ns], pages(cache_k), pages(cache_v), page_table, sb_bias)
    z_s, hre_s, him_s = _s5_step(u_s[:ns], state_re.reshape(ns, SSM_FLAT), state_im.reshape(ns, SSM_FLAT),
                                 s5w32, d_skip)
    pad_rows = lambda a: jnp.zeros((tb, a.shape[1]), a.dtype).at[:ns].set(a)
    h, xn, logits = _mix(pad_rows(attn_s), pad_rows(z_s), g_s, xs_pad, mix_w32, n_tot, n_p // tb, prev=outs)

    sf, blk, counts = _route(logits)
    n_tiles = (n_tot * TOP_K) // MOE_TILE + N_EXPERTS
    info, pad, tile_expert, tile_valid = _moe_plan(blk, counts, n_tiles)
    xs = _dispatch(info, pad, sf, xn, n_tiles * MOE_TILE)
    ys = _experts(tile_expert, tile_valid, xs, wu, bu, wd, bd)
    y_p, y_s = _combine(info, sf, h, row2(norm_final), ys, n_p)

    heads = (N_HEADS, HEAD_DIM)
    state = (N_GROUPS, SSM_STATE)
    time_major = lambda a, b_, t_: a.reshape(b_, N_HEADS, HEAD_DIM, t_).transpose(0, 3, 1, 2)[None]
    return (y_p.reshape(nb, nt, D_MODEL), y_s[:ns].reshape(ns, 1, D_MODEL),
            time_major(kt, nb, nt), time_major(vt, nb, nt),
            hre_p.reshape(1, nb, *state), him_p.reshape(1, nb, *state),
            time_major(kt_s[:, :, :ns], 1, ns).reshape(1, ns, 1, *heads),
            time_major(vt_s[:, :, :ns], 1, ns).reshape(1, ns, 1, *heads),
            hre_s.reshape(1, ns, *state), him_s.reshape(1, ns, *state))


def kernel(x_prompt, x_sample, cache_k, cache_v, state_ssm_re, state_ssm_im, page_table, norm_mix, w_in, sb_bias, w_attn_up, ssm_lambda_re, ssm_lambda_im, ssm_log_dt, ssm_b_re, ssm_b_im, ssm_c_re, ssm_c_im, ssm_d, glu_w1, glu_b1, glu_w2, glu_b2, w_out, norm_ffn, router_w, router_b, moe_w_up, moe_b_up, moe_w_down, moe_b_down, norm_final):
    return _forward(x_prompt, x_sample, cache_k[0], cache_v[0], state_ssm_re[0], state_ssm_im[0], page_table,
                    norm_mix[0], w_in[0], sb_bias[0], w_attn_up[0], ssm_lambda_re[0], ssm_lambda_im[0],
                    ssm_log_dt[0], ssm_b_re[0], ssm_b_im[0], ssm_c_re[0], ssm_c_im[0], ssm_d[0],
                    glu_w1[0], glu_b1[0], glu_w2[0], glu_b2[0], w_out[0], norm_ffn[0], router_w[0],
                    router_b[0], moe_w_up[0], moe_b_up[0], moe_w_down[0], moe_b_down[0], norm_final)
```

```python
import functools
import math

import jax
import jax.numpy as jnp
from jax import lax
from jax.experimental import pallas as pl
from jax.experimental.pallas import tpu as pltpu

F32 = jnp.float32
BF16 = jnp.bfloat16

D_MODEL = 1024
N_HEADS = 8
HEAD_DIM = 64
ATTN_WIDTH = N_HEADS * HEAD_DIM
SSM_WIDTH = 512
SSM_GROUP = 16
N_GROUPS = SSM_WIDTH // SSM_GROUP
SSM_STATE = 64
SSM_FLAT = N_GROUPS * SSM_STATE
N_EXPERTS = 32
TOP_K = 4
D_FF = 1024
SWIGLU_LIMIT = 7.0
SWIGLU_ALPHA = 1.702
RMS_EPS = 1e-5
PAGE_SIZE = 128
IN_WIDTH = 3 * ATTN_WIDTH + SSM_WIDTH + 2 * D_MODEL

LANES = 128
ROW_BLOCK = 256
ATTN_BLOCK = 256
HEADS_PER_STEP = 8
S5_CHUNKS = SSM_WIDTH // LANES
S5_TIME_BLOCK = 128
MOE_TILE = 512
MOE_FF_CHUNK = 256
PAGES_PER_STEP = 16
NEG_BIG = -1e30
VMEM_LIMIT = 56 * 1024 * 1024


def _dot(a, b):
    return jnp.dot(a, b, preferred_element_type=F32)


def _dot_nt(a, b):
    return lax.dot_general(a, b, (((1,), (1,)), ((), ())), preferred_element_type=F32)


def _dot_hp(a, b):
    return jnp.dot(a, b, preferred_element_type=F32, precision=lax.Precision.HIGHEST)


def _dot_nt_hp(a, b):
    return lax.dot_general(a, b, (((1,), (1,)), ((), ())), preferred_element_type=F32,
                           precision=lax.Precision.HIGHEST)


def _softplus(z):
    return jnp.maximum(z, 0.0) + jnp.log(1.0 + jnp.exp2(jnp.abs(z) * (-math.log2(math.e))))


def _split_bf16(x):
    hi = x.astype(BF16)
    lo = (x - hi.astype(F32)).astype(BF16)
    return hi, lo


def _params(sem, vmem=VMEM_LIMIT):
    return pltpu.CompilerParams(dimension_semantics=sem, vmem_limit_bytes=vmem)


def _inproj_body(x_ref, nw_ref, wq_ref, wkv_ref, wug_ref, q_ref, kt_ref, vt_ref, ktb_ref, vtb_ref, u_ref, g_ref,
                 *, precise):
    dot, dot_nt = (_dot_hp, _dot_nt_hp) if precise else (_dot, _dot_nt)
    x = x_ref[...]
    ms = jnp.mean(x * x, axis=-1, keepdims=True)
    xn = x * lax.rsqrt(ms + RMS_EPS) * nw_ref[...]
    if not precise:
        xn = xn.astype(BF16)
    a = ATTN_WIDTH
    q_ref[...] = (dot(xn, wq_ref[...]) * (HEAD_DIM ** -0.5)).astype(BF16)
    kvt = dot_nt(wkv_ref[...], xn)
    kt_ref[0] = kvt[:a]
    ktb_ref[0] = kvt[:a].astype(BF16)
    vt_ref[0] = kvt[a:]
    vtb_ref[0] = kvt[a:].astype(BF16)
    ug = dot(xn, wug_ref[...])
    u_ref[...] = ug[:, :SSM_WIDTH]
    g_ref[...] = ug[:, SSM_WIDTH:]


def _inproj(x2d, norm_w, w_in_bf, nb):
    n = x2d.shape[0]
    nt = n // nb
    tm = ROW_BLOCK
    nq = nt // tm
    a = ATTN_WIDTH
    row = lambda b, i: (b * nq + i, 0)
    full = lambda b, i: (0, 0)
    tmaj = lambda b, i: (b, 0, i)
    wq = w_in_bf[:, :a]
    wkv_t = w_in_bf[:, a:3 * a].T
    wug = w_in_bf[:, 3 * a:]
    outs = [
        jax.ShapeDtypeStruct((n, a), BF16),
        jax.ShapeDtypeStruct((nb, a, nt), F32),
        jax.ShapeDtypeStruct((nb, a, nt), F32),
        jax.ShapeDtypeStruct((nb, a, nt), BF16),
        jax.ShapeDtypeStruct((nb, a, nt), BF16),
        jax.ShapeDtypeStruct((n, SSM_WIDTH), F32),
        jax.ShapeDtypeStruct((n, 2 * D_MODEL), F32),
    ]
    return pl.pallas_call(
        functools.partial(_inproj_body, precise=w_in_bf.dtype == F32),
        out_shape=outs,
        grid=(nb, nq),
        in_specs=[pl.BlockSpec((tm, D_MODEL), row),
                  pl.BlockSpec((1, D_MODEL), full),
                  pl.BlockSpec(wq.shape, full), pl.BlockSpec(wkv_t.shape, full), pl.BlockSpec(wug.shape, full)],
        out_specs=[pl.BlockSpec((tm, a), row)] + [pl.BlockSpec((1, a, tm), tmaj)] * 4
                  + [pl.BlockSpec((tm, SSM_WIDTH), row), pl.BlockSpec((tm, 2 * D_MODEL), row)],
        compiler_params=_params(("parallel", "parallel")),
        name="inproj",
    )(x2d, norm_w, wq, wkv_t, wug)


def _attn_body(bias_ref, q_ref, k_ref, v_ref, o_ref, acc_ref):
    tq = tk = ATTN_BLOCK
    hg = pl.program_id(1)
    qi = pl.program_id(2)
    nh = HEADS_PER_STEP
    r = lax.broadcasted_iota(jnp.int32, (tq, tk), 0)
    c = lax.broadcasted_iota(jnp.int32, (tq, tk), 1)
    later = (r > c).astype(BF16)
    causal = c < r
    rows = [slice(h * HEAD_DIM, (h + 1) * HEAD_DIM) for h in range(nh)]
    extra = 16
    col = lax.broadcasted_iota(jnp.int32, (tq, extra), 1)
    ones2 = jnp.where(col < 2, 1.0, 0.0).astype(BF16)
    qh = [jnp.concatenate([q_ref[:, rows[h]], ones2], axis=1) for h in range(nh)]
    rowi = lax.broadcasted_iota(jnp.int32, (extra, tk), 0)
    bias_rows = []
    for h in range(nh):
        b = jnp.full((extra, tk), bias_ref[hg * nh + h], F32)
        b_hi = b.astype(BF16).astype(F32)
        bias_rows.append(jnp.where(rowi == 0, b_hi, jnp.where(rowi == 1, b - b_hi, 0.0)).astype(BF16))
    acc_ref[...] = jnp.zeros_like(acc_ref)

    def block(kb, carries, diagonal):
        start = pl.multiple_of(kb * tk, tk)
        kblk = k_ref[0, :, pl.ds(start, tk)]
        vblk = v_ref[0, :, pl.ds(start, tk)]
        zs = [_dot(qh[h], jnp.concatenate([kblk[rows[h], :], bias_rows[h]], axis=0)) for h in range(nh)]
        sps = [_softplus(z) for z in zs]
        if diagonal:
            sps = [jnp.where(causal, sp, 0.0) for sp in sps]
        newers = []
        for sp in sps:
            hi, lo = _split_bf16(sp)
            newers.append(_dot(hi, later) + _dot(lo, later))
        new = []
        for h in range(nh):
            w = jnp.exp(zs[h] - sps[h] - newers[h])
            if diagonal:
                w = jnp.where(causal, w, 0.0)
            pv = _dot_nt(w.astype(BF16), vblk[rows[h], :])
            acc_ref[h] += pv if diagonal else jnp.exp(-carries[h]) * pv
            new.append(carries[h] + newers[h][:, 0:1] + sps[h][:, 0:1])
        return tuple(new)

    zero = tuple(jnp.zeros((tq, 1), F32) for _ in range(nh))
    carries = block(qi, zero, True)
    odd = qi % 2
    carries = lax.cond(odd == 1, lambda cs: block(qi - 1, cs, False), lambda cs: cs, carries)
    top = qi - 1 - odd

    def pair(j, cs):
        return block(top - 2 * j - 1, block(top - 2 * j, cs, False), False)

    lax.fori_loop(0, qi // 2, pair, carries)
    for h in range(nh):
        o_ref[:, rows[h]] = acc_ref[h].astype(o_ref.dtype)


def _attn_prompt(q, ktb, vtb, bias, nb, nt):
    tq = ATTN_BLOCK
    nq = nt // tq
    width = HEADS_PER_STEP * HEAD_DIM
    ng = ATTN_WIDTH // width
    return pl.pallas_call(
        _attn_body,
        out_shape=jax.ShapeDtypeStruct((nb * nt, ATTN_WIDTH), BF16),
        grid=(nb, ng, nq),
        in_specs=[pl.BlockSpec(memory_space=pltpu.SMEM),
                  pl.BlockSpec((tq, width), lambda b, g, i: (b * nq + i, g)),
                  pl.BlockSpec((1, width, nt), lambda b, g, i: (b, g, 0)),
                  pl.BlockSpec((1, width, nt), lambda b, g, i: (b, g, 0))],
        out_specs=pl.BlockSpec((tq, width), lambda b, g, i: (b * nq + i, g)),
        scratch_shapes=[pltpu.VMEM((HEADS_PER_STEP, tq, HEAD_DIM), F32)],
        compiler_params=_params(("parallel", "parallel", "arbitrary")),
        name="attn_prompt",
    )(bias, q, ktb, vtb)


def _attn_sample_body(pt_ref, bias_ref, q_ref, *refs):
    np_ = PAGES_PER_STEP
    k_refs = refs[:np_]
    v_refs = refs[np_:2 * np_]
    o_ref, carry_ref, acc_ref = refs[2 * np_:]
    c = pl.program_id(1)

    @pl.when(c == 0)
    def _():
        carry_ref[...] = jnp.zeros_like(carry_ref)
        acc_ref[...] = jnp.zeros_like(acc_ref)

    head_of_lane = lax.broadcasted_iota(jnp.int32, (N_HEADS, ATTN_WIDTH), 1) // HEAD_DIM
    head_of_row = lax.broadcasted_iota(jnp.int32, (N_HEADS, ATTN_WIDTH), 0)
    own = head_of_lane == head_of_row
    qrow = q_ref[0].astype(F32)
    qbd = jnp.where(own, jnp.broadcast_to(qrow, (N_HEADS, ATTN_WIDTH)), 0.0).astype(BF16)
    r = lax.broadcasted_iota(jnp.int32, (PAGE_SIZE, PAGE_SIZE), 0)
    cc = lax.broadcasted_iota(jnp.int32, (PAGE_SIZE, PAGE_SIZE), 1)
    later = (r > cc).astype(BF16)
    bias = bias_ref[...]
    kcat = jnp.concatenate([k_refs[i][0].astype(BF16) for i in range(np_)], axis=1)
    vcat = jnp.concatenate([v_refs[i][0].astype(BF16) for i in range(np_)], axis=1)
    z = _dot(qbd, kcat) + bias
    sp = _softplus(z)
    page = lambda a, i: a[:, i * PAGE_SIZE:(i + 1) * PAGE_SIZE]
    sp_rows = jnp.concatenate([page(sp, i) for i in range(np_)], axis=0)
    hi, lo = _split_bf16(sp_rows)
    newer_rows = _dot(hi, later) + _dot(lo, later)
    carry = carry_ref[...]
    ws = []
    for i in range(np_):
        sp_i = page(sp, i)
        newer_i = newer_rows[i * N_HEADS:(i + 1) * N_HEADS, :]
        ws.append(jnp.exp(page(z, i) - sp_i - newer_i - carry))
        carry = carry + newer_i[:, 0:1] + sp_i[:, 0:1]
    carry_ref[...] = carry
    w = jnp.concatenate(ws, axis=1).astype(BF16)
    acc = acc_ref[...] + _dot_nt(w, vcat)
    acc_ref[...] = acc

    @pl.when(c == pl.num_programs(1) - 1)
    def _():
        o_ref[0] = jnp.sum(jnp.where(own, acc, 0.0), axis=0, keepdims=True).astype(o_ref.dtype)


def _attn_sample(q, cache_k, cache_v, page_table, bias):
    nseq, n_pages = page_table.shape
    np_ = PAGES_PER_STEP
    nchunk = n_pages // np_

    def page_map(i):
        def f(b, c, pt):
            return (pt[b * n_pages + (n_pages - 1 - (c * np_ + i))], 0, 0)
        return f

    page_spec = [pl.BlockSpec((1, ATTN_WIDTH, PAGE_SIZE), page_map(i)) for i in range(np_)]
    grid_spec = pltpu.PrefetchScalarGridSpec(
        num_scalar_prefetch=1,
        grid=(nseq, nchunk),
        in_specs=[pl.BlockSpec((N_HEADS, 1), lambda b, c, pt: (0, 0)),
                  pl.BlockSpec((1, 1, ATTN_WIDTH), lambda b, c, pt: (b, 0, 0))]
                 + page_spec + page_spec,
        out_specs=pl.BlockSpec((1, 1, ATTN_WIDTH), lambda b, c, pt: (b, 0, 0)),
        scratch_shapes=[pltpu.VMEM((N_HEADS, 1), F32), pltpu.VMEM((N_HEADS, ATTN_WIDTH), F32)],
    )
    out = pl.pallas_call(
        _attn_sample_body,
        out_shape=jax.ShapeDtypeStruct((nseq, 1, ATTN_WIDTH), BF16),
        grid_spec=grid_spec,
        compiler_params=_params(("parallel", "arbitrary")),
        name="attn_sample",
    )(page_table.reshape(-1), bias.reshape(N_HEADS, 1), q.reshape(nseq, 1, ATTN_WIDTH),
      *([cache_k] * np_), *([cache_v] * np_))
    return out.reshape(nseq, ATTN_WIDTH)


def _s5_prep_body(lr_ref, li_ref, ldt_ref, br_ref, bi_ref, ar_ref, ai_ref, bbr_ref, bbi_ref):
    lr = lr_ref[...]
    li = li_ref[...]
    dt = jnp.exp(ldt_ref[...])
    mag = jnp.exp(lr * dt)
    ar = mag * jnp.cos(li * dt)
    ai = mag * jnp.sin(li * dt)
    den = lr * lr + li * li
    nr = ar - 1.0
    ni = ai
    fr = (nr * lr + ni * li) / den
    fi = (ni * lr - nr * li) / den
    ar_ref[...] = ar
    ai_ref[...] = ai
    br = br_ref[...]
    bi = bi_ref[...]
    bbr_ref[...] = fr[:, None, :] * br - fi[:, None, :] * bi
    bbi_ref[...] = fr[:, None, :] * bi + fi[:, None, :] * br


def _s5_prep(lam_re, lam_im, log_dt, b_re, b_im):
    g, p, h = N_GROUPS, SSM_STATE, SSM_GROUP
    outs = [jax.ShapeDtypeStruct((g, p), F32)] * 2 + [jax.ShapeDtypeStruct((g, h, p), F32)] * 2
    return pl.pallas_call(_s5_prep_body, out_shape=outs, name="s5_prep")(
        lam_re, lam_im, log_dt.reshape(g, 1),
        jnp.swapaxes(b_re, 1, 2), jnp.swapaxes(b_im, 1, 2))


def _s5_weights(ar, ai, bbr_t, bbi_t, c_re, c_im, dtype):
    eye = jnp.eye(8, dtype=F32)

    def drive(bb_t):
        x = bb_t.reshape(S5_CHUNKS, 8, SSM_GROUP, SSM_STATE)
        return (x[:, :, :, None, :] * eye[None, :, None, :, None]).reshape(
            S5_CHUNKS, LANES, 8 * SSM_STATE).astype(dtype)

    def readout(cm):
        x = jnp.swapaxes(cm, 1, 2).reshape(S5_CHUNKS, 8, SSM_STATE, SSM_GROUP)
        return (x[:, :, :, None, :] * eye[None, :, None, :, None]).reshape(
            S5_CHUNKS, 8 * SSM_STATE, LANES).astype(dtype)

    return (ar.reshape(1, SSM_FLAT), ai.reshape(1, SSM_FLAT), drive(bbr_t), drive(bbi_t),
            readout(c_re), readout(-c_im))


def _gelu_tanh(x):
    return 0.5 * x * (1.0 + jnp.tanh(math.sqrt(2.0 / math.pi) * (x + 0.044715 * (x * x * x))))


def _s5_prompt_body(u_ref, ar_ref, ai_ref, bbr_ref, bbi_ref, cr_ref, nci_ref, d_ref,
                    z_ref, hre_ref, him_ref, hs_ref, st_ref, *, nb, tt, cw):
    i = pl.program_id(0)
    cs = 8 * SSM_STATE
    tiles_per_chunk = cs // LANES
    im0 = SSM_FLAT // LANES

    @pl.when(i == 0)
    def _():
        st_ref[...] = jnp.zeros_like(st_ref)

    for b in range(nb):
        rows = pl.ds(b, tt, stride=nb)
        for c in range(S5_CHUNKS):
            uc = u_ref[b, :, c * LANES:(c + 1) * LANES].astype(BF16)
            dre = _dot(uc, bbr_ref[c])
            dim = _dot(uc, bbi_ref[c])
            for m in range(tiles_per_chunk):
                lanes = slice(m * LANES, (m + 1) * LANES)
                hs_ref[c * tiles_per_chunk + m, rows, :] = dre[:, lanes]
                hs_ref[im0 + c * tiles_per_chunk + m, rows, :] = dim[:, lanes]

    nt_scan = cw // LANES
    for j in range(SSM_FLAT // cw):
        cols = slice(j * cw, (j + 1) * cw)
        ar = [jnp.broadcast_to(ar_ref[:, j * cw + m * LANES:j * cw + (m + 1) * LANES], (nb, LANES))
              for m in range(nt_scan)]
        ai = [jnp.broadcast_to(ai_ref[:, j * cw + m * LANES:j * cw + (m + 1) * LANES], (nb, LANES))
              for m in range(nt_scan)]

        def step(t, carry):
            rows = pl.ds(pl.multiple_of(t * nb, nb), nb)
            out = []
            for m in range(nt_scan):
                hr, hi = carry[2 * m], carry[2 * m + 1]
                tre = j * nt_scan + m
                nr = ar[m] * hr - ai[m] * hi + hs_ref[tre, rows, :]
                ni = ar[m] * hi + ai[m] * hr + hs_ref[im0 + tre, rows, :]
                hs_ref[tre, rows, :] = nr
                hs_ref[im0 + tre, rows, :] = ni
                out += [nr, ni]
            return tuple(out)

        init = []
        for m in range(nt_scan):
            lanes = slice(j * cw + m * LANES, j * cw + (m + 1) * LANES)
            init += [st_ref[0, :, lanes], st_ref[1, :, lanes]]
        fin = lax.fori_loop(0, tt, step, tuple(init), unroll=8)
        for m in range(nt_scan):
            lanes = slice(j * cw + m * LANES, j * cw + (m + 1) * LANES)
            st_ref[0, :, lanes] = fin[2 * m]
            st_ref[1, :, lanes] = fin[2 * m + 1]

    for b in range(nb):
        rows = pl.ds(b, tt, stride=nb)
        for c in range(S5_CHUNKS):
            t0 = c * tiles_per_chunk
            hre = jnp.concatenate([hs_ref[t0 + m, rows, :] for m in range(tiles_per_chunk)], axis=-1)
            him = jnp.concatenate([hs_ref[im0 + t0 + m, rows, :] for m in range(tiles_per_chunk)], axis=-1)
            lanes = slice(c * LANES, (c + 1) * LANES)
            y = (_dot(hre.astype(BF16), cr_ref[c]) + _dot(him.astype(BF16), nci_ref[c])
                 + d_ref[:, lanes] * u_ref[b, :, lanes])
            z_ref[b, :, lanes] = _gelu_tanh(y).astype(z_ref.dtype)

    @pl.when(i == pl.num_programs(0) - 1)
    def _():
        hre_ref[...] = st_ref[0]
        him_ref[...] = st_ref[1]


def _s5_prompt(u, weights, d_skip):
    nb, nt, _ = u.shape
    tt = min(S5_TIME_BLOCK, nt)
    ar, ai, bbr, bbi, cr, nci = weights
    full2 = lambda i: (0, 0)
    full3 = lambda i: (0, 0, 0)
    cs = 8 * SSM_STATE
    body = functools.partial(_s5_prompt_body, nb=nb, tt=tt, cw=512)
    return pl.pallas_call(
        body,
        out_shape=[jax.ShapeDtypeStruct((nb, nt, SSM_WIDTH), BF16),
                   jax.ShapeDtypeStruct((nb, SSM_FLAT), F32),
                   jax.ShapeDtypeStruct((nb, SSM_FLAT), F32)],
        grid=(nt // tt,),
        in_specs=[pl.BlockSpec((nb, tt, SSM_WIDTH), lambda i: (0, i, 0)),
                  pl.BlockSpec((1, SSM_FLAT), full2), pl.BlockSpec((1, SSM_FLAT), full2),
                  pl.BlockSpec((S5_CHUNKS, LANES, cs), full3), pl.BlockSpec((S5_CHUNKS, LANES, cs), full3),
                  pl.BlockSpec((S5_CHUNKS, cs, LANES), full3), pl.BlockSpec((S5_CHUNKS, cs, LANES), full3),
                  pl.BlockSpec((1, SSM_WIDTH), full2)],
        out_specs=[pl.BlockSpec((nb, tt, SSM_WIDTH), lambda i: (0, i, 0)),
                   pl.BlockSpec((nb, SSM_FLAT), full2), pl.BlockSpec((nb, SSM_FLAT), full2)],
        scratch_shapes=[pltpu.VMEM((2 * SSM_FLAT // LANES, tt * nb, LANES), F32),
                        pltpu.VMEM((2, nb, SSM_FLAT), F32)],
        compiler_params=_params(("arbitrary",)),
        name="s5_prompt",
    )(u, ar, ai, bbr, bbi, cr, nci, d_skip.reshape(1, SSM_WIDTH))


def _s5_step_body(u_ref, h0r_ref, h0i_ref, ar_ref, ai_ref, bbr_ref, bbi_ref, cr_ref, nci_ref, d_ref,
                  z_ref, hre_ref, him_ref):
    cs = 8 * SSM_STATE
    u = u_ref[...]
    for c in range(S5_CHUNKS):
        cols = slice(c * cs, (c + 1) * cs)
        lanes = slice(c * LANES, (c + 1) * LANES)
        uc = u[:, lanes]
        ar = ar_ref[:, cols]
        ai = ai_ref[:, cols]
        h0r = h0r_ref[:, cols]
        h0i = h0i_ref[:, cols]
        hr = ar * h0r - ai * h0i + _dot_hp(uc, bbr_ref[c])
        hi = ar * h0i + ai * h0r + _dot_hp(uc, bbi_ref[c])
        hre_ref[:, cols] = hr
        him_ref[:, cols] = hi
        y = _dot_hp(hr, cr_ref[c]) + _dot_hp(hi, nci_ref[c]) + d_ref[:, lanes] * uc
        z_ref[:, lanes] = _gelu_tanh(y)


def _s5_step(u, h0_re, h0_im, weights, d_skip):
    n = u.shape[0]
    ar, ai, bbr, bbi, cr, nci = weights
    return pl.pallas_call(
        _s5_step_body,
        out_shape=[jax.ShapeDtypeStruct((n, SSM_WIDTH), F32),
                   jax.ShapeDtypeStruct((n, SSM_FLAT), F32),
                   jax.ShapeDtypeStruct((n, SSM_FLAT), F32)],
        name="s5_step",
    )(u, h0_re, h0_im, ar, ai, bbr, bbi, cr, nci, d_skip.reshape(1, SSM_WIDTH))


def _mix_body(attn_ref, z_ref, g_ref, x_ref, wup_ref, w1_ref, b1_ref, w2_ref, b2_ref, wout_ref,
              nf_ref, rw_ref, rb_ref, *rest, n_blocks, precise):
    h_ref, xn_ref, lg_ref = rest[-3:]

    @pl.when(pl.program_id(0) >= n_blocks)
    def _():
        h_ref[...] = jnp.zeros_like(h_ref)
        xn_ref[...] = jnp.zeros_like(xn_ref)
        lg_ref[...] = jnp.zeros_like(lg_ref)

    pl.when(pl.program_id(0) < n_blocks)(functools.partial(
        _mix_rows, attn_ref, z_ref, g_ref, x_ref, wup_ref, w1_ref, b1_ref, w2_ref, b2_ref, wout_ref,
        nf_ref, rw_ref, rb_ref, h_ref, xn_ref, lg_ref, precise=precise))


def _mix_rows(attn_ref, z_ref, g_ref, x_ref, wup_ref, w1_ref, b1_ref, w2_ref, b2_ref, wout_ref,
              nf_ref, rw_ref, rb_ref, h_ref, xn_ref, lg_ref, *, precise):
    dot = _dot_hp if precise else _dot
    operand = (lambda a: a.astype(F32)) if precise else (lambda a: a.astype(BF16))
    y_attn = dot(operand(attn_ref[...]), wup_ref[...])
    z = operand(z_ref[...])
    y_ssm = (dot(z, w1_ref[...]) + b1_ref[...]) * jax.nn.sigmoid(dot(z, w2_ref[...]) + b2_ref[...])
    g = g_ref[...]
    merged = jax.nn.sigmoid(g[:, :D_MODEL]) * y_attn + jax.nn.sigmoid(g[:, D_MODEL:]) * y_ssm
    h = x_ref[...] + dot(operand(merged), wout_ref[...])
    h_ref[...] = h
    ms = jnp.mean(h * h, axis=-1, keepdims=True)
    xn = h * lax.rsqrt(ms + RMS_EPS) * nf_ref[...]
    xn_ref[...] = xn
    lg_ref[...] = dot(operand(xn), rw_ref[...]) + rb_ref[...]


def _mix(attn, z, g, x, w, n_total, first_block, prev=None):
    n = x.shape[0]
    tm = ROW_BLOCK
    n_blocks = n // tm
    n_steps = n_blocks if prev is not None else n_total // tm - first_block
    row = lambda i: (jnp.minimum(i, n_blocks - 1), 0)
    orow = lambda i: (i + first_block, 0)
    full = lambda i: (0, 0)
    wspecs = [pl.BlockSpec(a.shape, full) for a in w]
    in_specs = [pl.BlockSpec((tm, ATTN_WIDTH), row), pl.BlockSpec((tm, SSM_WIDTH), row),
                pl.BlockSpec((tm, 2 * D_MODEL), row), pl.BlockSpec((tm, D_MODEL), row)] + wspecs
    args = [attn, z, g, x, *w]
    aliases = {}
    if prev is not None:
        base = len(args)
        in_specs = in_specs + [pl.BlockSpec(memory_space=pl.ANY)] * 3
        args = args + list(prev)
        aliases = {base: 0, base + 1: 1, base + 2: 2}
    return pl.pallas_call(
        functools.partial(_mix_body, n_blocks=n_blocks, precise=w[0].dtype == F32),
        out_shape=[jax.ShapeDtypeStruct((n_total, D_MODEL), F32),
                   jax.ShapeDtypeStruct((n_total, D_MODEL), F32),
                   jax.ShapeDtypeStruct((n_total, LANES), F32)],
        grid=(n_steps,),
        in_specs=in_specs,
        out_specs=[pl.BlockSpec((tm, D_MODEL), orow), pl.BlockSpec((tm, D_MODEL), orow),
                   pl.BlockSpec((tm, LANES), orow)],
        input_output_aliases=aliases,
        compiler_params=_params(("parallel",)),
        name="mix",
    )(*args)


def _route_body(lg_ref, sf_ref, blk_ref, cnt_ref, run_ref):
    tb = ROW_BLOCK
    i = pl.program_id(0)

    @pl.when(i == 0)
    def _():
        run_ref[...] = jnp.zeros_like(run_ref)

    lg = lg_ref[...]
    lane = lax.broadcasted_iota(jnp.int32, (tb, LANES), 1).astype(F32)
    work = lg
    tops, hots, idxs = [], [], []
    for _ in range(TOP_K):
        m = jnp.max(work, axis=-1, keepdims=True)
        idx = jnp.min(jnp.where(work == m, lane, float(LANES)), axis=-1, keepdims=True)
        hot = lane == idx
        work = jnp.where(hot, -jnp.inf, work)
        tops.append(m)
        hots.append(hot)
        idxs.append(idx)
    es = [jnp.exp(m - tops[0]) for m in tops]
    den = es[0] + es[1] + es[2] + es[3]
    sel = jnp.zeros((tb, LANES), F32)
    for hot in hots:
        sel = jnp.where(hot, 1.0, sel)
    r = lax.broadcasted_iota(jnp.int32, (tb, tb), 0)
    c = lax.broadcasted_iota(jnp.int32, (tb, tb), 1)
    earlier = (c < r).astype(BF16)
    sel_bf = sel.astype(BF16)
    lrank = _dot(earlier, sel_bf)
    lcnt = jnp.sum(sel, axis=0, keepdims=True)
    below = (lax.broadcasted_iota(jnp.int32, (LANES, LANES), 0)
             < lax.broadcasted_iota(jnp.int32, (LANES, LANES), 1)).astype(BF16)
    loff = _dot(jnp.broadcast_to(lcnt, (16, LANES)).astype(BF16), below)[0:1]
    lpos = lrank + loff
    sf = jnp.zeros((tb, LANES), F32)
    for k in range(TOP_K):
        pk = jnp.sum(jnp.where(hots[k], lpos, 0.0), axis=-1, keepdims=True)
        sf = jnp.where(lane == float(k), es[k] / den, sf)
        sf = jnp.where(lane == float(TOP_K + k), pk, sf)
    sf_ref[...] = sf
    blk_ref[0] = jnp.concatenate([lcnt, run_ref[...]] + [jnp.zeros((1, LANES), F32)] * 6, axis=0)
    run = run_ref[...] + lcnt
    run_ref[...] = run
    cnt_ref[...] = run


def _route(logits):
    n = logits.shape[0]
    tb = ROW_BLOCK
    row = lambda i: (i, 0)
    return pl.pallas_call(
        _route_body,
        out_shape=[jax.ShapeDtypeStruct((n, LANES), F32),
                   jax.ShapeDtypeStruct((n // tb, 8, LANES), F32),
                   jax.ShapeDtypeStruct((1, LANES), F32)],
        grid=(n // tb,),
        in_specs=[pl.BlockSpec((tb, LANES), row)],
        out_specs=[pl.BlockSpec((tb, LANES), row), pl.BlockSpec((1, 8, LANES), lambda i: (i, 0, 0)),
                   pl.BlockSpec((1, LANES), lambda i: (0, 0))],
        scratch_shapes=[pltpu.VMEM((1, LANES), F32)],
        compiler_params=_params(("arbitrary",)),
        name="route",
    )(logits)


ROW_TILE = D_MODEL // LANES
PAIRS = ROW_BLOCK * TOP_K
SEG_BITS = tuple(1 << b for b in range(ROW_BLOCK.bit_length() - 1, -1, -1))
PAD_BITS = tuple(1 << b for b in range(MOE_TILE.bit_length() - 2, -1, -1))


def _segment_copies(length, bits, make_copy):
    for n, bit in enumerate(bits):
        @pl.when((length & bit) != 0)
        def _(n=n, bit=bit):
            make_copy(length & ~(2 * bit - 1), bit, n % 2)


def _tile_rows(ref, first_row, n_rows):
    start = first_row * ROW_TILE
    if not isinstance(start, int):
        start = pl.multiple_of(start, ROW_TILE)
    return ref.at[pl.ds(start, n_rows * ROW_TILE)]


def _to_row_tiles(ref, value):
    n = value.shape[0]
    for s in range(ROW_TILE):
        ref[pl.ds(s, n, stride=ROW_TILE), :] = value[:, s * LANES:(s + 1) * LANES]


def _from_row_tiles(ref, n):
    return jnp.concatenate([ref[pl.ds(s, n, stride=ROW_TILE), :] for s in range(ROW_TILE)], axis=1)


def _dispatch_body(info_ref, pad_ref, sf_ref, x_ref, xs_ref, xl0_ref, xl1_ref, sem0, sem1):
    tb = ROW_BLOCK
    i = pl.program_id(0)

    @pl.when(i == 0)
    def _():
        xl1_ref[...] = jnp.zeros_like(xl1_ref)

        def fill(e, _):
            start = pad_ref[e]

            def copy(off, size, queue):
                del queue
                cp = pltpu.make_async_copy(_tile_rows(xl1_ref, 0, size), _tile_rows(xs_ref, start + off, size), sem1)
                cp.start()
                cp.wait()

            _segment_copies(pad_ref[N_EXPERTS + e], PAD_BITS, copy)
            return 0

        lax.fori_loop(0, N_EXPERTS, fill, 0)

        def clear_tile(t, _):
            cp = pltpu.make_async_copy(_tile_rows(xl1_ref, 0, MOE_TILE), _tile_rows(xs_ref, t * MOE_TILE, MOE_TILE), sem1)
            cp.start()
            cp.wait()
            return 0

        lax.fori_loop(pad_ref[2 * N_EXPERTS], xs_ref.shape[0] // (MOE_TILE * ROW_TILE), clear_tile, 0)

    def all_landed(xl_ref, sem):
        pltpu.make_async_copy(xl_ref, _tile_rows(xs_ref, 0, PAIRS), sem).wait()

    def step(xl_ref, sem, other_ref, other_sem):
        lpos_t = jnp.transpose(sf_ref[...])
        slot = lax.broadcasted_iota(jnp.int32, (PAIRS, tb), 0).astype(F32)
        onehot = jnp.zeros((PAIRS, tb), F32)
        for k in range(TOP_K):
            onehot = onehot + jnp.where(slot == lpos_t[TOP_K + k:TOP_K + k + 1, :], 1.0, 0.0)
        _to_row_tiles(xl_ref, _dot(onehot.astype(BF16), x_ref[...].astype(BF16)))

        def segment(e, _):
            length, src, dst = info_ref[0, 0, e], info_ref[0, 0, N_EXPERTS + e], info_ref[0, 0, 2 * N_EXPERTS + e]
            _segment_copies(length, SEG_BITS, lambda off, size, queue: pltpu.make_async_copy(
                _tile_rows(xl_ref, src + off, size), _tile_rows(xs_ref, dst + off, size), sem).start(queue))
            return 0

        lax.fori_loop(0, N_EXPERTS, segment, 0)

        @pl.when(i > 0)
        def _():
            all_landed(other_ref, other_sem)

        @pl.when(i == pl.num_programs(0) - 1)
        def _():
            all_landed(xl_ref, sem)

    pl.when(i % 2 == 0)(functools.partial(step, xl0_ref, sem0, xl1_ref, sem1))
    pl.when(i % 2 == 1)(functools.partial(step, xl1_ref, sem1, xl0_ref, sem0))


def _dispatch(info, pad, sf, xn, n_sorted):
    n = xn.shape[0]
    tb = ROW_BLOCK
    row = lambda i: (i, 0)
    return pl.pallas_call(
        _dispatch_body,
        out_shape=jax.ShapeDtypeStruct((n_sorted * ROW_TILE, LANES), F32),
        grid=(n // tb,),
        in_specs=[pl.BlockSpec((1, 1, LANES), lambda i: (i, 0, 0), memory_space=pltpu.SMEM),
                  pl.BlockSpec(memory_space=pltpu.SMEM),
                  pl.BlockSpec((tb, LANES), row),
                  pl.BlockSpec((tb, D_MODEL), row)],
        out_specs=pl.BlockSpec(memory_space=pl.ANY),
        scratch_shapes=[pltpu.VMEM((PAIRS * ROW_TILE, LANES), F32), pltpu.VMEM((PAIRS * ROW_TILE, LANES), F32),
                        pltpu.SemaphoreType.DMA, pltpu.SemaphoreType.DMA],
        compiler_params=_params(("arbitrary",)),
        name="dispatch",
    )(info, pad, sf, xn)


def _combine_body(info_ref, next_ref, sf_ref, h_ref, nw_ref, ys_ref, yp_ref, ys_out_ref,
                  yl0_ref, yl1_ref, sem0, sem1, *, n_prompt_blocks):
    tb = ROW_BLOCK
    i = pl.program_id(0)

    def fetch(blk_ref, yl_ref, sem):
        def segment(e, _):
            length, dst, src = blk_ref[0, 0, e], blk_ref[0, 0, N_EXPERTS + e], blk_ref[0, 0, 2 * N_EXPERTS + e]
            _segment_copies(length, SEG_BITS, lambda off, size, queue: pltpu.make_async_copy(
                _tile_rows(ys_ref, src + off, size), _tile_rows(yl_ref, dst + off, size), sem).start(queue))
            return 0

        lax.fori_loop(0, N_EXPERTS, segment, 0)

    def step(yl_ref, sem, other_ref, other_sem):
        @pl.when(i == 0)
        def _():
            fetch(info_ref, yl_ref, sem)

        @pl.when(i + 1 < pl.num_programs(0))
        def _():
            fetch(next_ref, other_ref, other_sem)

        sf = sf_ref[...]
        pltpu.make_async_copy(_tile_rows(ys_ref, 0, PAIRS), yl_ref, sem).wait()
        out = h_ref[...]
        for c in range(TOP_K):
            slot = (lax.broadcasted_iota(jnp.int32, (tb, tb), 1) + c * tb).astype(F32)
            gates = jnp.zeros((tb, tb), F32)
            for k in range(TOP_K):
                gates = gates + jnp.where(slot == sf[:, TOP_K + k:TOP_K + k + 1], sf[:, k:k + 1], 0.0)
            g_hi, g_lo = _split_bf16(gates)
            y_hi, y_lo = _split_bf16(_from_row_tiles(_tile_rows(yl_ref, c * tb, tb), tb))
            out = out + (_dot(g_hi, y_hi) + _dot(g_lo, y_hi) + _dot(g_hi, y_lo))
        ms = jnp.mean(out * out, axis=-1, keepdims=True)
        y = out * lax.rsqrt(ms + RMS_EPS) * nw_ref[...]

        @pl.when(i < n_prompt_blocks)
        def _():
            yp_ref[...] = y

        @pl.when(i >= n_prompt_blocks)
        def _():
            ys_out_ref[...] = y

    pl.when(i % 2 == 0)(functools.partial(step, yl0_ref, sem0, yl1_ref, sem1))
    pl.when(i % 2 == 1)(functools.partial(step, yl1_ref, sem1, yl0_ref, sem0))


def _combine(info, sf, h, norm_w, ys, n_prompt):
    n = h.shape[0]
    tb = ROW_BLOCK
    nblk = n // tb
    npb = n_prompt // tb
    row = lambda i: (i, 0)
    return pl.pallas_call(
        functools.partial(_combine_body, n_prompt_blocks=npb),
        out_shape=[jax.ShapeDtypeStruct((n_prompt, D_MODEL), F32),
                   jax.ShapeDtypeStruct((n - n_prompt, D_MODEL), F32)],
        grid=(nblk,),
        in_specs=[pl.BlockSpec((1, 1, LANES), lambda i: (i, 0, 0), memory_space=pltpu.SMEM),
                  pl.BlockSpec((1, 1, LANES), lambda i: (jnp.minimum(i + 1, nblk - 1), 0, 0),
                               memory_space=pltpu.SMEM),
                  pl.BlockSpec((tb, LANES), row),
                  pl.BlockSpec((tb, D_MODEL), row),
                  pl.BlockSpec((1, D_MODEL), lambda i: (0, 0)),
                  pl.BlockSpec(memory_space=pl.ANY)],
        out_specs=[pl.BlockSpec((tb, D_MODEL), lambda i: (jnp.minimum(i, npb - 1), 0)),
                   pl.BlockSpec((tb, D_MODEL), lambda i: (jnp.maximum(i - npb, 0), 0))],
        scratch_shapes=[pltpu.VMEM((PAIRS * ROW_TILE, LANES), F32), pltpu.VMEM((PAIRS * ROW_TILE, LANES), F32),
                        pltpu.SemaphoreType.DMA, pltpu.SemaphoreType.DMA],
        compiler_params=_params(("arbitrary",)),
        name="combine",
    )(info, info, sf, h, norm_w, ys)


def _experts_body(te_ref, tv_ref, xs_ref, wu_ref, bu_ref, wd_ref, bd_ref, ys_ref, stage_ref, wdb_ref, wub_ref):
    i = pl.program_id(0)

    @pl.when((i == 0) | (te_ref[i] != te_ref[jnp.maximum(i - 1, 0)]))
    def _():
        wub_ref[...] = wu_ref[0].astype(BF16)
        half = D_FF // 2
        for c in range(ROW_TILE):
            lanes = slice(c * LANES, (c + 1) * LANES)
            stage_ref[c, pl.ds(0, half, stride=2), :] = wd_ref[0, :half, lanes]
            stage_ref[c, pl.ds(1, half, stride=2), :] = wd_ref[0, half:, lanes]
        wdb_ref[...] = jnp.concatenate([stage_ref[c] for c in range(ROW_TILE)], axis=1).astype(BF16)

    @pl.when(tv_ref[i] == 0)
    def _():
        ys_ref[...] = jnp.zeros_like(ys_ref)

    @pl.when(tv_ref[i] != 0)
    def _():
        x = _from_row_tiles(xs_ref, MOE_TILE).astype(BF16)
        even = lax.broadcasted_iota(jnp.int32, (MOE_TILE, LANES), 1) % 2 == 0
        acts = []
        for f in range(D_FF // MOE_FF_CHUNK):
            c1 = slice(f * MOE_FF_CHUNK, (f + 1) * MOE_FF_CHUNK)
            c2 = slice(D_FF + f * MOE_FF_CHUNK, D_FF + (f + 1) * MOE_FF_CHUNK)
            h1 = _dot(x, wub_ref[:, c1]) + bu_ref[0, :, c1]
            h2 = _dot(x, wub_ref[:, c2]) + bu_ref[0, :, c2]
            glu, lin = [], []
            for m in range(MOE_FF_CHUNK // LANES):
                a = h1[:, m * LANES:(m + 1) * LANES]
                b = h2[:, m * LANES:(m + 1) * LANES]
                glu.append(jnp.where(even, a, pltpu.roll(b, 1, 1)))
                lin.append(jnp.where(even, pltpu.roll(a, LANES - 1, 1), b))
            x_glu = jnp.minimum(jnp.concatenate(glu, axis=-1), SWIGLU_LIMIT)
            x_lin = jnp.clip(jnp.concatenate(lin, axis=-1), -SWIGLU_LIMIT, SWIGLU_LIMIT)
            acts.append((x_glu * jax.nn.sigmoid(SWIGLU_ALPHA * x_glu) * (x_lin + 1.0)).astype(BF16))
        _to_row_tiles(ys_ref, _dot(jnp.concatenate(acts, axis=-1), wdb_ref[...]) + bd_ref[0])


def _experts(tile_expert, tile_valid, xs, wu, bu, wd, bd):
    n_tiles = tile_expert.shape[0]
    tm = MOE_TILE
    wmap = lambda i, te, tv: (te[i], 0, 0)
    grid_spec = pltpu.PrefetchScalarGridSpec(
        num_scalar_prefetch=2,
        grid=(n_tiles,),
        in_specs=[pl.BlockSpec((tm * ROW_TILE, LANES), lambda i, te, tv: (i * tv[i], 0)),
                  pl.BlockSpec((1, D_MODEL, 2 * D_FF), wmap), pl.BlockSpec((1, 1, 2 * D_FF), wmap),
                  pl.BlockSpec((1, D_FF, D_MODEL), wmap), pl.BlockSpec((1, 1, D_MODEL), wmap)],
        out_specs=pl.BlockSpec((tm * ROW_TILE, LANES), lambda i, te, tv: (i, 0)),
        scratch_shapes=[pltpu.VMEM((ROW_TILE, D_FF, LANES), F32), pltpu.VMEM((D_FF, D_MODEL), BF16),
                        pltpu.VMEM((D_MODEL, 2 * D_FF), BF16)],
    )
    return pl.pallas_call(
        _experts_body,
        out_shape=jax.ShapeDtypeStruct((n_tiles * tm * ROW_TILE, LANES), F32),
        grid_spec=grid_spec,
        compiler_params=_params(("arbitrary",)),
        name="experts",
    )(tile_expert, tile_valid, xs, wu, bu, wd, bd)


def _moe_plan(blk, counts, n_tiles):
    as_int = lambda a: a.astype(jnp.int32)
    seg_len = as_int(blk[:, 0, :N_EXPERTS])
    before = as_int(blk[:, 1, :N_EXPERTS])
    cnt = as_int(counts[0, :N_EXPERTS])
    tiles = (cnt + MOE_TILE - 1) // MOE_TILE
    ends = jnp.cumsum(tiles)
    offs = (ends - tiles) * MOE_TILE
    local = jnp.cumsum(seg_len, axis=1) - seg_len
    info = jnp.concatenate([seg_len, local, offs[None, :] + before,
                            jnp.zeros((seg_len.shape[0], LANES - 3 * N_EXPERTS), jnp.int32)], axis=1)
    pad = jnp.concatenate([offs + cnt, tiles * MOE_TILE - cnt, ends[-1:]])
    tile_id = jnp.arange(n_tiles, dtype=jnp.int32)
    tile_expert = jnp.minimum(jnp.sum((tile_id[:, None] >= ends[None, :]).astype(jnp.int32), axis=1),
                              N_EXPERTS - 1)
    tile_valid = (tile_id < ends[-1]).astype(jnp.int32)
    return info[:, None, :], pad, tile_expert, tile_valid


def _forward(x_prompt, x_sample, cache_k, cache_v, state_re, state_im, page_table,
             norm_mix, w_in, sb_bias, w_attn_up, lam_re, lam_im, log_dt, b_re, b_im, c_re, c_im,
             d_skip, glu_w1, glu_b1, glu_w2, glu_b2, w_out, norm_ffn, router_w, router_b,
             moe_w_up, moe_b_up, moe_w_down, moe_b_down, norm_final):
    nb, nt, _ = x_prompt.shape
    ns = x_sample.shape[0]
    n_p = nb * nt
    tb = ROW_BLOCK
    n_tot = n_p + tb
    row2 = lambda a: a.reshape(1, -1)

    w_in_bf = w_in.astype(BF16)
    mix_w32 = [w_attn_up, glu_w1, row2(glu_b1), glu_w2, row2(glu_b2), w_out, row2(norm_ffn),
               jnp.pad(router_w, ((0, 0), (0, LANES - N_EXPERTS))),
               jnp.pad(row2(router_b), ((0, 0), (0, LANES - N_EXPERTS)), constant_values=NEG_BIG)]
    mix_w = [a.astype(BF16) if a.shape[0] > 1 else a for a in mix_w32]
    wu = moe_w_up
    bu = moe_b_up[:, None, :]
    wd = moe_w_down
    bd = moe_b_down[:, None, :]
    ar, ai, bbr_t, bbi_t = _s5_prep(lam_re, lam_im, log_dt, b_re, b_im)
    s5w = _s5_weights(ar, ai, bbr_t, bbi_t, c_re, c_im, BF16)
    s5w32 = _s5_weights(ar, ai, bbr_t, bbi_t, c_re, c_im, F32)

    xp = x_prompt.reshape(n_p, D_MODEL)
    q, kt, vt, ktb, vtb, u, g = _inproj(xp, row2(norm_mix), w_in_bf, nb)
    attn = _attn_prompt(q, ktb, vtb, sb_bias, nb, nt)
    z, hre_p, him_p = _s5_prompt(u.reshape(nb, nt, SSM_WIDTH), s5w, d_skip)
    outs = _mix(attn, z.reshape(n_p, SSM_WIDTH), g, xp, mix_w, n_tot, 0)

    xs_pad = jnp.zeros((tb, D_MODEL), F32).at[:ns].set(x_sample.reshape(ns, D_MODEL))
    q_s, kt_s, vt_s, _, _, u_s, g_s = _inproj(xs_pad, row2(norm_mix), w_in, 1)
    n_pool = cache_k.shape[0]
    pages = lambda c: c.transpose(0, 2, 3, 1).reshape(n_pool, ATTN_WIDTH, PAGE_SIZE)
    attn_s = _attn_sample(q_s[:ns], pages(cache_k), pages(cache_v), page_table, sb_bias)
    z_s, hre_s, him_s = _s5_step(u_s[:ns], state_re.reshape(ns, SSM_FLAT), state_im.reshape(ns, SSM_FLAT),
                                 s5w32, d_skip)
    pad_rows = lambda a: jnp.zeros((tb, a.shape[1]), a.dtype).at[:ns].set(a)
    h, xn, logits = _mix(pad_rows(attn_s), pad_rows(z_s), g_s, xs_pad, mix_w32, n_tot, n_p // tb, prev=outs)

    sf, blk, counts = _route(logits)
    n_tiles = (n_tot * TOP_K) // MOE_TILE + N_EXPERTS
    info, pad, tile_expert, tile_valid = _moe_plan(blk, counts, n_tiles)
    xs = _dispatch(info, pad, sf, xn, n_tiles * MOE_TILE)
    ys = _experts(tile_expert, tile_valid, xs, wu, bu, wd, bd)
    y_p, y_s = _combine(info, sf, h, row2(norm_final), ys, n_p)

    heads = (N_HEADS, HEAD_DIM)
    state = (N_GROUPS, SSM_STATE)
    time_major = lambda a, b_, t_: a.reshape(b_, N_HEADS, HEAD_DIM, t_).transpose(0, 3, 1, 2)[None]
    return (y_p.reshape(nb, nt, D_MODEL), y_s[:ns].reshape(ns, 1, D_MODEL),
            time_major(kt, nb, nt), time_major(vt, nb, nt),
            hre_p.reshape(1, nb, *state), him_p.reshape(1, nb, *state),
            time_major(kt_s[:, :, :ns], 1, ns).reshape(1, ns, 1, *heads),
            time_major(vt_s[:, :, :ns], 1, ns).reshape(1, ns, 1, *heads),
            hre_s.reshape(1, ns, *state), him_s.reshape(1, ns, *state))


def kernel(x_prompt, x_sample, cache_k, cache_v, state_ssm_re, state_ssm_im, page_table, norm_mix, w_in, sb_bias, w_attn_up, ssm_lambda_re, ssm_lambda_im, ssm_log_dt, ssm_b_re, ssm_b_im, ssm_c_re, ssm_c_im, ssm_d, glu_w1, glu_b1, glu_w2, glu_b2, w_out, norm_ffn, router_w, router_b, moe_w_up, moe_b_up, moe_w_down, moe_b_down, norm_final):
    return _forward(x_prompt, x_sample, cache_k[0], cache_v[0], state_ssm_re[0], state_ssm_im[0], page_table,
                    norm_mix[0], w_in[0], sb_bias[0], w_attn_up[0], ssm_lambda_re[0], ssm_lambda_im[0],
                    ssm_log_dt[0], ssm_b_re[0], ssm_b_im[0], ssm_c_re[0], ssm_c_im[0], ssm_d[0],
                    glu_w1[0], glu_b1[0], glu_w2[0], glu_b2[0], w_out[0], norm_ffn[0], router_w[0],
                    router_b[0], moe_w_up[0], moe_b_up[0], moe_w_down[0], moe_b_down[0], norm_final)
```
